```python
import jax, jax.numpy as jnp
from jax import lax
import numpy as np

D_MODEL = 2048
BATCH = 4
SEQ = 4096
DEPTH = 1

N_META = 16
D_LRU = D_MODEL // 2
N_LRU_HEADS = 16
LRU_BLOCK = D_LRU // N_LRU_HEADS
LRU_CONV_WIDTH = 4
LRU_C = 8.0
D_SCONV = D_MODEL - D_LRU
N_SCONV_GROUPS = 16
SCONV_BLOCK = D_SCONV // N_SCONV_GROUPS
SCONV_WIDTH = 3
D_FF = ((8 * D_MODEL // 3 + 127) // 128) * 128
IN_COLS = 2 * D_LRU + 3 * D_SCONV
EPS = 1e-6

kernel_name = "hymba_lru_shortconv_macaron"


def rmsnorm(x, g):
    xf = x.astype(jnp.float32)
    y = xf * lax.rsqrt(jnp.mean(xf * xf, axis=-1, keepdims=True) + EPS)
    return (y * g.astype(jnp.float32)).astype(x.dtype)


def group_rmsnorm(x, g, n_groups):
    b, t, c = x.shape
    xf = x.astype(jnp.float32).reshape(b, t, n_groups, c // n_groups)
    y = xf * lax.rsqrt(jnp.mean(xf * xf, axis=-1, keepdims=True) + EPS)
    return (y.reshape(b, t, c) * g.astype(jnp.float32)).astype(x.dtype)


def causal_depthwise_conv(x, w):
    k = w.shape[0]
    return lax.conv_general_dilated(
        x, w[:, None, :].astype(x.dtype), window_strides=(1,),
        padding=[(k - 1, 0)], dimension_numbers=("NWC", "WIO", "NWC"),
        feature_group_count=x.shape[-1])


def swiglu(x, w_gate, w_up, w_down):
    return (jax.nn.silu(x @ w_gate) * (x @ w_up)) @ w_down


def rg_lru(x, w_a, b_a, w_x, b_x, lam):
    bn, t, c = x.shape
    xh = x.reshape(bn, t, N_LRU_HEADS, LRU_BLOCK)
    gate_a = jax.nn.sigmoid(jnp.einsum("bthi,hij->bthj", xh, w_a).reshape(bn, t, c) + b_a)
    gate_x = jax.nn.sigmoid(jnp.einsum("bthi,hij->bthj", xh, w_x).reshape(bn, t, c) + b_x)
    log_a = -LRU_C * gate_a.astype(jnp.float32) * jax.nn.softplus(-lam.astype(jnp.float32))
    a = jnp.exp(log_a)
    mult = jnp.sqrt(-jnp.expm1(2.0 * log_a))
    u = mult * (gate_x * x).astype(jnp.float32)

    def combine(left, right):
        a_l, b_l = left
        a_r, b_r = right
        return a_r * a_l, a_r * b_l + b_r

    _, h = lax.associative_scan(combine, (a, u), axis=1)
    return h.astype(x.dtype)


def setup_inputs(seed: int = 0) -> dict:
    key = jax.random.key(seed)
    ks = iter(jax.random.split(key, 40))
    f32 = jnp.float32
    L = DEPTH

    def nrm(shape, fan_in):
        return jax.random.normal(next(ks), shape, f32) * (fan_in ** -0.5)

    def gain(shape):
        return 1.0 + 0.02 * jax.random.normal(next(ks), shape, f32)

    def bias(shape):
        return 0.01 * jax.random.normal(next(ks), shape, f32)

    x = jax.random.normal(next(ks), (BATCH, SEQ, D_MODEL), f32)
    meta_tokens = jax.random.normal(next(ks), (N_META, D_MODEL), f32)

    a_c = jax.random.uniform(next(ks), (L, D_LRU), f32, 0.9, 0.999)
    s = a_c ** (1.0 / LRU_C)
    lru_lambda = jnp.log(s) - jnp.log1p(-s)

    return {
        "x": x,
        "meta_tokens": meta_tokens,
        "ffn1_pre_g": gain((L, D_MODEL)),
        "ffn1_w_gate": nrm((L, D_MODEL, D_FF), D_MODEL),
        "ffn1_w_up": nrm((L, D_MODEL, D_FF), D_MODEL),
        "ffn1_w_down": nrm((L, D_FF, D_MODEL), D_FF),
        "ffn1_post_g": gain((L, D_MODEL)),
        "mix_pre_g": gain((L, D_MODEL)),
        "w_in": nrm((L, D_MODEL, IN_COLS), D_MODEL),
        "lru_conv_w": nrm((L, LRU_CONV_WIDTH, D_LRU), LRU_CONV_WIDTH),
        "lru_conv_b": bias((L, D_LRU)),
        "lru_w_a": nrm((L, N_LRU_HEADS, LRU_BLOCK, LRU_BLOCK), LRU_BLOCK),
        "lru_b_a": bias((L, D_LRU)),
        "lru_w_x": nrm((L, N_LRU_HEADS, LRU_BLOCK, LRU_BLOCK), LRU_BLOCK),
        "lru_b_x": bias((L, D_LRU)),
        "lru_lambda": lru_lambda,
        "sconv_w": nrm((L, SCONV_WIDTH, D_SCONV), SCONV_WIDTH),
        "lru_out_g": gain((L, D_LRU)),
        "sconv_out_g": gain((L, D_SCONV)),
        "w_out": nrm((L, D_MODEL, D_MODEL), D_MODEL),
        "mix_post_g": gain((L, D_MODEL)),
        "ffn2_pre_g": gain((L, D_MODEL)),
        "ffn2_w_gate": nrm((L, D_MODEL, D_FF), D_MODEL),
        "ffn2_w_up": nrm((L, D_MODEL, D_FF), D_MODEL),
        "ffn2_w_down": nrm((L, D_FF, D_MODEL), D_FF),
        "ffn2_post_g": gain((L, D_MODEL)),
    }


def reference(x, meta_tokens, ffn1_pre_g, ffn1_w_gate, ffn1_w_up, ffn1_w_down, ffn1_post_g,
              mix_pre_g, w_in, lru_conv_w, lru_conv_b, lru_w_a, lru_b_a, lru_w_x, lru_b_x,
              lru_lambda, sconv_w, lru_out_g, sconv_out_g, w_out, mix_post_g,
              ffn2_pre_g, ffn2_w_gate, ffn2_w_up, ffn2_w_down, ffn2_post_g):
    bn = x.shape[0]
    meta = jnp.broadcast_to(meta_tokens.astype(x.dtype)[None], (bn, N_META, x.shape[-1]))
    h = jnp.concatenate([meta, x], axis=1)
    splits = [D_LRU, 2 * D_LRU, 2 * D_LRU + D_SCONV, 2 * D_LRU + 2 * D_SCONV]

    for l in range(DEPTH):
        f = swiglu(rmsnorm(h, ffn1_pre_g[l]), ffn1_w_gate[l], ffn1_w_up[l], ffn1_w_down[l])
        h = h + 0.5 * rmsnorm(f, ffn1_post_g[l])

        u = rmsnorm(h, mix_pre_g[l])
        z = u @ w_in[l]
        y_lru, x_lru, b_sc, c_sc, v_sc = jnp.split(z, splits, axis=-1)

        x_lru = causal_depthwise_conv(x_lru, lru_conv_w[l]) + lru_conv_b[l]
        lru_out = rg_lru(x_lru, lru_w_a[l], lru_b_a[l], lru_w_x[l], lru_b_x[l], lru_lambda[l])
        lru_out = lru_out * jax.nn.gelu(y_lru, approximate=True)

        sc_out = b_sc * causal_depthwise_conv(c_sc * v_sc, sconv_w[l])

        mixed = jnp.concatenate([
            group_rmsnorm(lru_out, lru_out_g[l], N_LRU_HEADS),
            group_rmsnorm(sc_out, sconv_out_g[l], N_SCONV_GROUPS)], axis=-1)
        h = h + rmsnorm(mixed @ w_out[l], mix_post_g[l])

        f = swiglu(rmsnorm(h, ffn2_pre_g[l]), ffn2_w_gate[l], ffn2_w_up[l], ffn2_w_down[l])
        h = h + 0.5 * rmsnorm(f, ffn2_post_g[l])

    return h[:, N_META:]
```

```python
import functools

import jax
import jax.numpy as jnp
from jax import lax
from jax.experimental import pallas as pl
from jax.experimental.pallas import tpu as pltpu

EPS = 1e-6
N_LRU_HEADS = 16
N_SCONV_GROUPS = 16
LRU_C = 8.0
LANE = 128
SUBLANE = 8
MXU_DIM = 256
VMEM_LIMIT = 56 * 1024 * 1024

F32 = jnp.float32
BF16 = jnp.bfloat16


def _rms(x, g):
    ms = jnp.mean(x * x, axis=-1, keepdims=True)
    return x * lax.rsqrt(ms + EPS) * g


def _ffn_kernel(h_ref, pre_g_ref, wg_ref, wu_ref, wd_ref, post_g_ref, o_ref, xn_ref):
    j = pl.program_id(1)

    @pl.when(j == 0)
    def _():
        xn_ref[...] = _rms(h_ref[...], pre_g_ref[...]).astype(BF16)
        o_ref[...] = jnp.zeros_like(o_ref)

    xn = xn_ref[...]
    g = jnp.dot(xn, wg_ref[...], preferred_element_type=F32)
    u = jnp.dot(xn, wu_ref[...], preferred_element_type=F32)
    a = (g * jax.nn.sigmoid(g) * u).astype(BF16)
    o_ref[...] += jnp.dot(a, wd_ref[...], preferred_element_type=F32)

    @pl.when(j == pl.num_programs(1) - 1)
    def _():
        o_ref[...] = h_ref[...] + 0.5 * _rms(o_ref[...], post_g_ref[...])


def _ffn(h, pre_g, wg, wu, wd, post_g, *, tm, tf):
    m, d = h.shape
    f = wg.shape[1]
    return pl.pallas_call(
        _ffn_kernel,
        grid=(m // tm, f // tf),
        in_specs=[
            pl.BlockSpec((tm, d), lambda i, j: (i, 0)),
            pl.BlockSpec((1, d), lambda i, j: (0, 0)),
            pl.BlockSpec((d, tf), lambda i, j: (0, j)),
            pl.BlockSpec((d, tf), lambda i, j: (0, j)),
            pl.BlockSpec((tf, d), lambda i, j: (j, 0)),
            pl.BlockSpec((1, d), lambda i, j: (0, 0)),
        ],
        out_specs=pl.BlockSpec((tm, d), lambda i, j: (i, 0)),
        out_shape=jax.ShapeDtypeStruct((m, d), F32),
        scratch_shapes=[pltpu.VMEM((tm, d), BF16)],
        compiler_params=pltpu.CompilerParams(
            dimension_semantics=("parallel", "arbitrary"), vmem_limit_bytes=VMEM_LIMIT),
        name="ffn",
    )(h, pre_g, wg, wu, wd, post_g)


def _inproj_kernel(h_ref, g_ref, w_ref, z_ref, xn_ref):
    @pl.when(pl.program_id(1) == 0)
    def _():
        xn_ref[...] = _rms(h_ref[...], g_ref[...]).astype(BF16)

    z_ref[...] = jnp.dot(xn_ref[...], w_ref[...], preferred_element_type=F32)


def _inproj(h, g, w, *, tm, tn):
    m, d = h.shape
    n = w.shape[1]
    return pl.pallas_call(
        _inproj_kernel,
        grid=(m // tm, n // tn),
        in_specs=[
            pl.BlockSpec((tm, d), lambda i, j: (i, 0)),
            pl.BlockSpec((1, d), lambda i, j: (0, 0)),
            pl.BlockSpec((d, tn), lambda i, j: (0, j)),
        ],
        out_specs=pl.BlockSpec((tm, tn), lambda i, j: (i, j)),
        out_shape=jax.ShapeDtypeStruct((m, n), F32),
        scratch_shapes=[pltpu.VMEM((tm, d), BF16)],
        compiler_params=pltpu.CompilerParams(
            dimension_semantics=("parallel", "arbitrary"), vmem_limit_bytes=VMEM_LIMIT),
        name="inproj",
    )(h, g, w)


def _shift_rows(x, k, head):
    r = pltpu.roll(x, k, 0)
    hr = pltpu.roll(head, k, 0)
    row = lax.broadcasted_iota(jnp.int32, head.shape, 0)
    first = jnp.where(row < k, hr, r[:SUBLANE])
    if x.shape[0] == SUBLANE:
        return first
    return jnp.concatenate([first, r[SUBLANE:]], axis=0)


def _shift_fill(x, s, fill):
    tt = x.shape[0]
    if s >= SUBLANE:
        return jnp.concatenate([jnp.full((s,) + x.shape[1:], fill, x.dtype), x[: tt - s]], axis=0)
    r = pltpu.roll(x, s, 0)
    row = lax.broadcasted_iota(jnp.int32, (SUBLANE,) + x.shape[1:], 0)
    first = jnp.where(row < s, fill, r[:SUBLANE])
    if tt == SUBLANE:
        return first
    return jnp.concatenate([first, r[SUBLANE:]], axis=0)


def _scan(a, u, h_in):
    tt = a.shape[0]
    s = 1
    while s < tt:
        u = u + a * _shift_fill(u, s, 0.0)
        a = a * _shift_fill(a, s, 1.0)
        s *= 2
    return u + a * h_in


def _group_mean_sq(y, gmat):
    y2 = y * y
    hi = y2.astype(BF16)
    lo = (y2 - hi.astype(F32)).astype(BF16)
    s = jnp.dot(hi, gmat, preferred_element_type=F32) + jnp.dot(lo, gmat, preferred_element_type=F32)
    return s * (1.0 / 64.0)


def _gelu_tanh(x):
    return 0.5 * x * (1.0 + jnp.tanh(0.7978845608028654 * (x + 0.044715 * (x * x * x))))


def _mixer_kernel(z_ref, cw_ref, cb_ref, wax_ref, ba_ref, bx_ref, lam_ref, sw_ref, lg_ref, sg_ref,
                  gmat_ref, xh0_ref, cvh0_ref, h0_ref,
                  mixed_ref, xt_ref, cvt_ref, hl_ref,
                  xh_s, cvh_s, hc_s, *, d_lru, d_sc):
    @pl.when(pl.program_id(1) == 0)
    def _():
        xh_s[...] = xh0_ref[...]
        cvh_s[...] = cvh0_ref[...]
        hc_s[...] = h0_ref[...]

    tt = z_ref.shape[0]
    gmat = gmat_ref[...]
    w = MXU_DIM
    for c in range(d_lru // w):
        c0 = c * w
        cols = slice(c0, c0 + w)
        x = z_ref[:, d_lru + c0:d_lru + c0 + w]
        head = xh_s[:, cols]
        xc = (cb_ref[:, cols] + cw_ref[3:4, cols] * x
              + cw_ref[2:3, cols] * _shift_rows(x, 1, head)
              + cw_ref[1:2, cols] * _shift_rows(x, 2, head)
              + cw_ref[0:1, cols] * _shift_rows(x, 3, head))
        xh_s[:, cols] = x[tt - SUBLANE:]
        pre = jnp.dot(xc.astype(BF16), wax_ref[c], preferred_element_type=F32)
        ga = jax.nn.sigmoid(pre[:, :w] + ba_ref[:, cols])
        gx = jax.nn.sigmoid(pre[:, w:] + bx_ref[:, cols])
        nl = -lam_ref[:, cols]
        sp = jnp.maximum(nl, 0.0) + jnp.log1p(jnp.exp(-jnp.abs(nl)))
        log_a = (-LRU_C * ga) * sp
        a = jnp.exp(log_a)
        mult = jnp.sqrt(-jnp.tanh(log_a) * (a * a + 1.0))
        u = mult * (gx * xc)
        h = _scan(a, u, hc_s[:, cols])
        hc_s[:, cols] = h[tt - 1:]
        lo = h * _gelu_tanh(z_ref[:, cols])
        lo = lo * lax.rsqrt(_group_mean_sq(lo, gmat) + EPS) * lg_ref[:, cols]
        mixed_ref[:, cols] = lo.astype(mixed_ref.dtype)
        o1 = 2 * d_lru + c0
        o2 = o1 + d_sc
        o3 = o2 + d_sc
        cv = z_ref[:, o2:o2 + w] * z_ref[:, o3:o3 + w]
        chead = cvh_s[:, cols]
        sc = z_ref[:, o1:o1 + w] * (sw_ref[2:3, cols] * cv
                                    + sw_ref[1:2, cols] * _shift_rows(cv, 1, chead)
                                    + sw_ref[0:1, cols] * _shift_rows(cv, 2, chead))
        cvh_s[:, cols] = cv[tt - SUBLANE:]
        sc = sc * lax.rsqrt(_group_mean_sq(sc, gmat) + EPS) * sg_ref[:, cols]
        mixed_ref[:, d_lru + c0:d_lru + c0 + w] = sc.astype(mixed_ref.dtype)

    xt_ref[...] = xh_s[...]
    cvt_ref[...] = cvh_s[...]
    hl_ref[0] = hc_s[...]


def _mixer(z, p, xh0, cvh0, h0, *, nb, tt):
    rows, _ = z.shape
    d_lru = p["cb"].shape[1]
    d_sc = p["sw"].shape[1]
    nt = rows // (nb * tt)
    const = lambda b, t: (0, 0)
    vec = pl.BlockSpec((1, d_lru), const)
    assert d_lru == d_sc, "one column loop serves both branches"
    state8 = jax.ShapeDtypeStruct((nb * SUBLANE, d_lru), F32)
    return pl.pallas_call(
        functools.partial(_mixer_kernel, d_lru=d_lru, d_sc=d_sc),
        grid=(nb, nt),
        in_specs=[
            pl.BlockSpec((tt, z.shape[1]), lambda b, t: (b * nt + t, 0)),
            pl.BlockSpec(p["cw"].shape, const), vec,
            pl.BlockSpec(p["wax"].shape, lambda b, t: (0, 0, 0)),
            vec, vec, vec,
            pl.BlockSpec(p["sw"].shape, const), vec, vec,
            pl.BlockSpec((MXU_DIM, MXU_DIM), const),
            pl.BlockSpec((SUBLANE, d_lru), const),
            pl.BlockSpec((SUBLANE, d_sc), const),
            vec,
        ],
        out_specs=[
            pl.BlockSpec((tt, d_lru + d_sc), lambda b, t: (b * nt + t, 0)),
            pl.BlockSpec((SUBLANE, d_lru), lambda b, t: (b, 0)),
            pl.BlockSpec((SUBLANE, d_sc), lambda b, t: (b, 0)),
            pl.BlockSpec((1, 1, d_lru), lambda b, t: (b, 0, 0)),
        ],
        out_shape=[
            jax.ShapeDtypeStruct((rows, d_lru + d_sc), BF16),
            state8, state8,
            jax.ShapeDtypeStruct((nb, 1, d_lru), F32),
        ],
        scratch_shapes=[
            pltpu.VMEM((SUBLANE, d_lru), F32),
            pltpu.VMEM((SUBLANE, d_sc), F32),
            pltpu.VMEM((1, d_lru), F32),
        ],
        compiler_params=pltpu.CompilerParams(
            dimension_semantics=("parallel", "arbitrary"), vmem_limit_bytes=VMEM_LIMIT),
        name="mixer",
    )(z, p["cw"], p["cb"], p["wax"], p["ba"], p["bx"], p["lam"], p["sw"], p["lg"], p["sg"],
      p["gmat"], xh0, cvh0, h0)


def _outproj_kernel(m_ref, w_ref, g_ref, h_ref, o_ref):
    y = jnp.dot(m_ref[...], w_ref[...], preferred_element_type=F32)
    o_ref[...] = h_ref[...] + _rms(y, g_ref[...])


def _outproj(mixed, w, g, h, *, tm):
    m, d = h.shape
    return pl.pallas_call(
        _outproj_kernel,
        grid=(m // tm,),
        in_specs=[
            pl.BlockSpec((tm, d), lambda i: (i, 0)),
            pl.BlockSpec(w.shape, lambda i: (0, 0)),
            pl.BlockSpec((1, d), lambda i: (0, 0)),
            pl.BlockSpec((tm, d), lambda i: (i, 0)),
        ],
        out_specs=pl.BlockSpec((tm, d), lambda i: (i, 0)),
        out_shape=jax.ShapeDtypeStruct((m, d), F32),
        compiler_params=pltpu.CompilerParams(
            dimension_semantics=("parallel",), vmem_limit_bytes=VMEM_LIMIT),
        name="outproj",
    )(mixed, w, g, h)


def _block_diag_gates(w_a, w_x):
    nh, blk, _ = w_a.shape
    per = MXU_DIM // blk
    eye = jnp.eye(per, dtype=w_a.dtype)

    def bd(w):
        w = w.reshape(nh // per, per, blk, blk)
        return jnp.einsum("cpij,pq->cpiqj", w, eye).reshape(nh // per, MXU_DIM, MXU_DIM)

    return jnp.concatenate([bd(w_a), bd(w_x)], axis=-1).astype(BF16)


def _pad_cols(w, n):
    return jnp.pad(w, ((0, 0), (0, n - w.shape[1])))


def kernel(x, meta_tokens, ffn1_pre_g, ffn1_w_gate, ffn1_w_up, ffn1_w_down, ffn1_post_g, mix_pre_g, w_in,
           lru_conv_w, lru_conv_b, lru_w_a, lru_b_a, lru_w_x, lru_b_x, lru_lambda, sconv_w, lru_out_g,
           sconv_out_g, w_out, mix_post_g, ffn2_pre_g, ffn2_w_gate, ffn2_w_up, ffn2_w_down, ffn2_post_g):
    bn, seq, d = x.shape
    n_meta = meta_tokens.shape[0]
    d_lru = lru_conv_w.shape[-1]
    d_sc = sconv_w.shape[-1]
    d_ff = ffn1_w_gate.shape[-1]
    tf = 512
    f_pad = -(-d_ff // tf) * tf
    assert ffn1_pre_g.shape[0] == 1, "single layer"
    assert n_meta % SUBLANE == 0 and n_meta >= SUBLANE
    assert lru_w_a.shape[-1] * N_LRU_HEADS == d_lru and d_sc // N_SCONV_GROUPS == 64 and d_lru // N_LRU_HEADS == 64

    def ffn_weights(wg, wu, wd):
        return (_pad_cols(wg[0], f_pad).astype(BF16), _pad_cols(wu[0], f_pad).astype(BF16),
                jnp.pad(wd[0], ((0, f_pad - d_ff), (0, 0))).astype(BF16))

    w1 = ffn_weights(ffn1_w_gate, ffn1_w_up, ffn1_w_down)
    w2 = ffn_weights(ffn2_w_gate, ffn2_w_up, ffn2_w_down)
    w_in_b = w_in[0].astype(BF16)
    w_out_b = w_out[0].astype(BF16)
    grp = jnp.arange(MXU_DIM) // 64
    mp = {
        "cw": lru_conv_w[0], "cb": lru_conv_b, "wax": _block_diag_gates(lru_w_a[0], lru_w_x[0]),
        "ba": lru_b_a, "bx": lru_b_x, "lam": lru_lambda, "sw": sconv_w[0], "lg": lru_out_g, "sg": sconv_out_g,
        "gmat": (grp[:, None] == grp[None, :]).astype(BF16),
    }

    hm = _ffn(meta_tokens, ffn1_pre_g, *w1, ffn1_post_g, tm=n_meta, tf=tf)
    zm = _inproj(hm, mix_pre_g, w_in_b, tm=n_meta, tn=512)
    zero8 = jnp.zeros((SUBLANE, d_lru), F32)
    _, xt, cvt, hl = _mixer(zm, mp, zero8, zero8, jnp.zeros((1, d_lru), F32), nb=1, tt=n_meta)

    h = x.reshape(bn * seq, d)
    h = _ffn(h, ffn1_pre_g, *w1, ffn1_post_g, tm=512, tf=tf)
    z = _inproj(h, mix_pre_g, w_in_b, tm=512, tn=512)
    mixed, _, _, _ = _mixer(z, mp, xt, cvt, hl[0], nb=bn, tt=256)
    h = _outproj(mixed, w_out_b, mix_post_g, h, tm=512)
    h = _ffn(h, ffn2_pre_g, *w2, ffn2_post_g, tm=512, tf=tf)
    return h.reshape(bn, seq, d)
```

```python
import functools

import jax
import jax.numpy as jnp
from jax import lax
from jax.experimental import pallas as pl
from jax.experimental.pallas import tpu as pltpu

EPS = 1e-6
N_LRU_HEADS = 16
N_SCONV_GROUPS = 16
HEAD_DIM = 64
LRU_C = 8.0
SUBLANE = 8
MXU_DIM = 256
VMEM_LIMIT = 56 * 1024 * 1024
N_Z = 5

F32 = jnp.float32
BF16 = jnp.bfloat16


def _sigmoid(x):
    return 0.5 + 0.5 * jnp.tanh(0.5 * x)


def _rms(x, g):
    ms = jnp.mean(x * x, axis=-1, keepdims=True)
    return x * lax.rsqrt(ms + EPS) * g


def _ffn_step(h_ref, pre_g_ref, wg_ref, wu_ref, wd_ref, post_g_ref, o_ref, xn_ref, *, first, last, width, rc):
    tm = h_ref.shape[0]
    for r in range(tm // rc):
        rows = pl.ds(r * rc, rc)
        if first:
            xn = _rms(h_ref[rows, :], pre_g_ref[...]).astype(BF16)
            xn_ref[rows, :] = xn
        else:
            xn = xn_ref[rows, :]
        g = jnp.dot(xn, wg_ref[:, :width], preferred_element_type=F32)
        u = jnp.dot(xn, wu_ref[:, :width], preferred_element_type=F32)
        a = (g * _sigmoid(g) * u).astype(BF16)
        acc = jnp.dot(a, wd_ref[:width, :], preferred_element_type=F32)
        if not first:
            acc = o_ref[rows, :] + acc
        if last:
            acc = h_ref[rows, :] + 0.5 * _rms(acc, post_g_ref[...])
        o_ref[rows, :] = acc


def _ffn_kernel(h_ref, pre_g_ref, wg_ref, wu_ref, wd_ref, post_g_ref, o_ref, xn_ref, *, rem, rc):
    j = pl.program_id(1)
    nj = pl.num_programs(1)
    refs = (h_ref, pre_g_ref, wg_ref, wu_ref, wd_ref, post_g_ref, o_ref, xn_ref)
    tf = wg_ref.shape[1]

    @pl.when(j == 0)
    def _():
        _ffn_step(*refs, first=True, last=False, width=tf, rc=rc)

    @pl.when(jnp.logical_and(j > 0, j < nj - 1))
    def _():
        _ffn_step(*refs, first=False, last=False, width=tf, rc=rc)

    @pl.when(j == nj - 1)
    def _():
        _ffn_step(*refs, first=False, last=True, width=rem, rc=rc)


def _ffn(h, pre_g, wg, wu, wd, post_g, *, tm, tf, rc):
    m, d = h.shape
    f = wg.shape[1]
    nj = pl.cdiv(f, tf)
    rem = f - (nj - 1) * tf
    assert nj >= 2 and rem % 128 == 0 and m % tm == 0 and tm % rc == 0
    return pl.pallas_call(
        functools.partial(_ffn_kernel, rem=rem, rc=rc),
        grid=(m // tm, nj),
        in_specs=[
            pl.BlockSpec((tm, d), lambda i, j: (i, 0)),
            pl.BlockSpec((1, d), lambda i, j: (0, 0)),
            pl.BlockSpec((d, tf), lambda i, j: (0, j)),
            pl.BlockSpec((d, tf), lambda i, j: (0, j)),
            pl.BlockSpec((tf, d), lambda i, j: (j, 0)),
            pl.BlockSpec((1, d), lambda i, j: (0, 0)),
        ],
        out_specs=pl.BlockSpec((tm, d), lambda i, j: (i, 0)),
        out_shape=jax.ShapeDtypeStruct((m, d), F32),
        scratch_shapes=[pltpu.VMEM((tm, d), BF16)],
        compiler_params=pltpu.CompilerParams(
            dimension_semantics=("parallel", "arbitrary"), vmem_limit_bytes=VMEM_LIMIT),
        name="ffn",
    )(h, pre_g, wg, wu, wd, post_g)


def _shift_rows(x, k, head):
    r = pltpu.roll(x, k, 0)
    hr = pltpu.roll(head, k, 0)
    row = lax.broadcasted_iota(jnp.int32, head.shape, 0)
    first = jnp.where(row < k, hr, r[:SUBLANE])
    if x.shape[0] == SUBLANE:
        return first
    return jnp.concatenate([first, r[SUBLANE:]], axis=0)


def _shift_fill(x, s, fill):
    tt = x.shape[0]
    r = pltpu.roll(x, s, 0)
    row = lax.broadcasted_iota(jnp.int32, (SUBLANE,) + x.shape[1:], 0)
    first = jnp.where(row < s, fill, r[:SUBLANE])
    if tt == SUBLANE:
        return first
    return jnp.concatenate([first, r[SUBLANE:]], axis=0)


def _window_steps(a, u, steps):
    for s in steps:
        u = u + a * _shift_fill(u, s, 0.0)
        a = a * _shift_fill(a, s, 1.0)
    return a, u


def _scan_groups(a, u, h_in):
    hs = []
    h = h_in
    for g in range(a.shape[0] // SUBLANE):
        rows = slice(g * SUBLANE, (g + 1) * SUBLANE)
        h = a[rows] * h + u[rows]
        hs.append(h)
    return jnp.concatenate(hs, axis=0)


def _group_mean_sq(y, gmat):
    y2 = y * y
    hi = y2.astype(BF16)
    lo = (y2 - hi.astype(F32)).astype(BF16)
    s = jnp.dot(hi, gmat, preferred_element_type=F32) + jnp.dot(lo, gmat, preferred_element_type=F32)
    return s * (1.0 / HEAD_DIM)


def _gelu_tanh(x):
    return 0.5 * x * (1.0 + jnp.tanh(0.7978845608028654 * (x + 0.044715 * (x * x * x))))


def _mix_kernel(h_ref, pre_g_ref, win_ref, cw_ref, cb_ref, wax_ref, ba_ref, bx_ref, lam_ref, sw_ref, lg_ref,
                sg_ref, gmat_ref, wout_ref, post_g_ref, xh0_ref, cvh0_ref, h0_ref,
                o_ref, xt_ref, cvt_ref, hl_ref,
                xh_s, cvh_s, hc_s, un_s, z0_s, z1_s, m0_s, m1_s, acc_s, *, d_lru):
    @pl.when(pl.program_id(1) == 0)
    def _():
        xh_s[...] = xh0_ref[...]
        cvh_s[...] = cvh0_ref[...]
        hc_s[...] = h0_ref[...]

    tt, d = h_ref.shape
    w = MXU_DIM
    nc = d_lru // w
    z_slots = (z0_s, z1_s)
    m_slots = (m0_s, m1_s)
    gmat = gmat_ref[...]
    un_s[...] = _rms(h_ref[...], pre_g_ref[...]).astype(BF16)

    def project_pieces(c):
        def piece(k):
            def run():
                col = (c * N_Z + k) * w
                z_slots[c % 2][:, k * w:(k + 1) * w] = jnp.dot(
                    un_s[...], win_ref[:, col:col + w], preferred_element_type=F32)
            return run
        return [piece(k) for k in range(N_Z)]

    def outproj_pieces(c):
        half = d // 2

        def piece(n):
            def run():
                ncols = slice(n * half, (n + 1) * half)
                part = jnp.dot(m_slots[c % 2][...], wout_ref[c * 2 * w:(c + 1) * 2 * w, ncols],
                               preferred_element_type=F32)
                if c == 0:
                    acc_s[:, ncols] = part
                else:
                    acc_s[:, ncols] += part
            return run
        return [piece(0), piece(1)]

    def mixer(c):
        z = z_slots[c % 2]
        m = m_slots[c % 2]
        cols = slice(c * w, (c + 1) * w)
        x = z[:, w:2 * w]
        head = xh_s[:, cols]
        xc = (cb_ref[:, cols] + cw_ref[3:4, cols] * x
              + cw_ref[2:3, cols] * _shift_rows(x, 1, head)
              + cw_ref[1:2, cols] * _shift_rows(x, 2, head)
              + cw_ref[0:1, cols] * _shift_rows(x, 3, head))
        xh_s[:, cols] = x[tt - SUBLANE:]
        pre = jnp.dot(xc.astype(BF16), wax_ref[c], preferred_element_type=F32)
        yield
        cv = z[:, 3 * w:4 * w] * z[:, 4 * w:5 * w]
        chead = cvh_s[:, cols]
        sc = z[:, 2 * w:3 * w] * (sw_ref[2:3, cols] * cv
                                  + sw_ref[1:2, cols] * _shift_rows(cv, 1, chead)
                                  + sw_ref[0:1, cols] * _shift_rows(cv, 2, chead))
        cvh_s[:, cols] = cv[tt - SUBLANE:]
        sc_ms = _group_mean_sq(sc, gmat)
        yield
        ga = _sigmoid(pre[:, :w] + ba_ref[:, cols])
        gx = _sigmoid(pre[:, w:] + bx_ref[:, cols])
        nl = -lam_ref[:, cols]
        sp = jnp.maximum(nl, 0.0) + jnp.log1p(jnp.exp(-jnp.abs(nl)))
        log_a = (-LRU_C * ga) * sp
        a = jnp.exp(log_a)
        mult = jnp.sqrt(-jnp.tanh(log_a) * (a * a + 1.0))
        u = mult * (gx * xc)
        yield
        a, u = _window_steps(a, u, (1, 2))
        yield
        a, u = _window_steps(a, u, (4,))
        h = _scan_groups(a, u, hc_s[:, cols])
        hc_s[:, cols] = h[tt - 1:]
        yield
        lo = h * _gelu_tanh(z[:, 0:w])
        lo_ms = _group_mean_sq(lo, gmat)
        yield
        m[:, w:] = (sc * lax.rsqrt(sc_ms + EPS) * sg_ref[:, cols]).astype(BF16)
        m[:, :w] = (lo * lax.rsqrt(lo_ms + EPS) * lg_ref[:, cols]).astype(BF16)
        yield

    for run in project_pieces(0):
        run()
    pending = []
    for c in range(nc):
        if c + 1 < nc:
            pending += project_pieces(c + 1)
        for _ in mixer(c):
            if pending:
                pending.pop(0)()
        while pending:
            pending.pop(0)()
        pending = outproj_pieces(c)
    for run in pending:
        run()

    o_ref[...] = h_ref[...] + _rms(acc_s[...], post_g_ref[...])
    xt_ref[...] = xh_s[...]
    cvt_ref[...] = cvh_s[...]
    hl_ref[0] = hc_s[...]


def _mix(h, p, xh0, cvh0, h0, *, nb, tt):
    rows, d = h.shape
    d_lru = p["cb"].shape[1]
    nt = rows // (nb * tt)
    assert rows == nb * nt * tt
    const = lambda b, t: (0, 0)
    once = pl.Buffered(1)
    vec = pl.BlockSpec((1, d_lru), const)
    gvec = pl.BlockSpec((1, d), const)
    state8 = jax.ShapeDtypeStruct((nb * SUBLANE, d_lru), F32)
    return pl.pallas_call(
        functools.partial(_mix_kernel, d_lru=d_lru),
        grid=(nb, nt),
        in_specs=[
            pl.BlockSpec((tt, d), lambda b, t: (b * nt + t, 0)),
            gvec,
            pl.BlockSpec(p["win"].shape, const, pipeline_mode=once),
            pl.BlockSpec(p["cw"].shape, const), vec,
            pl.BlockSpec(p["wax"].shape, lambda b, t: (0, 0, 0)),
            vec, vec, vec,
            pl.BlockSpec(p["sw"].shape, const), vec, vec,
            pl.BlockSpec((MXU_DIM, MXU_DIM), const),
            pl.BlockSpec(p["wout"].shape, const, pipeline_mode=once),
            gvec,
            pl.BlockSpec((SUBLANE, d_lru), const),
            pl.BlockSpec((SUBLANE, d_lru), const),
            vec,
        ],
        out_specs=[
            pl.BlockSpec((tt, d), lambda b, t: (b * nt + t, 0)),
            pl.BlockSpec((SUBLANE, d_lru), lambda b, t: (b, 0)),
            pl.BlockSpec((SUBLANE, d_lru), lambda b, t: (b, 0)),
            pl.BlockSpec((1, 1, d_lru), lambda b, t: (b, 0, 0)),
        ],
        out_shape=[
            jax.ShapeDtypeStruct((rows, d), F32),
            state8, state8,
            jax.ShapeDtypeStruct((nb, 1, d_lru), F32),
        ],
        scratch_shapes=[
            pltpu.VMEM((SUBLANE, d_lru), F32),
            pltpu.VMEM((SUBLANE, d_lru), F32),
            pltpu.VMEM((1, d_lru), F32),
            pltpu.VMEM((tt, d), BF16),
            pltpu.VMEM((tt, N_Z * MXU_DIM), F32),
            pltpu.VMEM((tt, N_Z * MXU_DIM), F32),
            pltpu.VMEM((tt, 2 * MXU_DIM), BF16),
            pltpu.VMEM((tt, 2 * MXU_DIM), BF16),
            pltpu.VMEM((tt, d), F32),
        ],
        compiler_params=pltpu.CompilerParams(
            dimension_semantics=("parallel", "arbitrary"), vmem_limit_bytes=VMEM_LIMIT),
        name="mix",
    )(h, p["pre_g"], p["win"], p["cw"], p["cb"], p["wax"], p["ba"], p["bx"], p["lam"], p["sw"], p["lg"],
      p["sg"], p["gmat"], p["wout"], p["post_g"], xh0, cvh0, h0)


def _block_diag_gates(w_a, w_x):
    nh, blk, _ = w_a.shape
    per = MXU_DIM // blk
    eye = jnp.eye(per, dtype=w_a.dtype)

    def bd(w):
        w = w.reshape(nh // per, per, blk, blk)
        return jnp.einsum("cpij,pq->cpiqj", w, eye).reshape(nh // per, MXU_DIM, MXU_DIM)

    return jnp.concatenate([bd(w_a), bd(w_x)], axis=-1).astype(BF16)


def kernel(x, meta_tokens, ffn1_pre_g, ffn1_w_gate, ffn1_w_up, ffn1_w_down, ffn1_post_g, mix_pre_g, w_in,
           lru_conv_w, lru_conv_b, lru_w_a, lru_b_a, lru_w_x, lru_b_x, lru_lambda, sconv_w, lru_out_g,
           sconv_out_g, w_out, mix_post_g, ffn2_pre_g, ffn2_w_gate, ffn2_w_up, ffn2_w_down, ffn2_post_g):
    bn, seq, d = x.shape
    n_meta = meta_tokens.shape[0]
    d_lru = lru_conv_w.shape[-1]
    d_sc = sconv_w.shape[-1]
    assert ffn1_pre_g.shape[0] == 1, "single layer"
    assert n_meta % SUBLANE == 0 and n_meta >= SUBLANE
    assert d_lru == d_sc == N_LRU_HEADS * HEAD_DIM == N_SCONV_GROUPS * HEAD_DIM
    assert lru_w_a.shape[-1] == HEAD_DIM and d_lru % MXU_DIM == 0
    assert w_in.shape[-1] == N_Z * d_lru and w_out.shape[1] == 2 * d_lru

    w1 = tuple(w[0].astype(BF16) for w in (ffn1_w_gate, ffn1_w_up, ffn1_w_down))
    w2 = tuple(w[0].astype(BF16) for w in (ffn2_w_gate, ffn2_w_up, ffn2_w_down))
    nc = d_lru // MXU_DIM
    head = jnp.arange(MXU_DIM) // HEAD_DIM
    mp = {
        "pre_g": mix_pre_g,
        "win": w_in[0].reshape(d, N_Z, nc, MXU_DIM).transpose(0, 2, 1, 3).reshape(d, N_Z * d_lru).astype(BF16),
        "cw": lru_conv_w[0], "cb": lru_conv_b,
        "wax": _block_diag_gates(lru_w_a[0], lru_w_x[0]), "ba": lru_b_a, "bx": lru_b_x, "lam": lru_lambda,
        "sw": sconv_w[0], "lg": lru_out_g, "sg": sconv_out_g,
        "gmat": (head[:, None] == head[None, :]).astype(BF16),
        "wout": w_out[0].reshape(2, nc, MXU_DIM, d).transpose(1, 0, 2, 3).reshape(2 * d_lru, d).astype(BF16),
        "post_g": mix_post_g,
    }
    ffn_tiles = dict(tf=1024, rc=256)

    hm = _ffn(meta_tokens, ffn1_pre_g, *w1, ffn1_post_g, tm=n_meta, rc=n_meta, tf=ffn_tiles["tf"])
    zero8 = jnp.zeros((SUBLANE, d_lru), F32)
    _, xt, cvt, hl = _mix(hm, mp, zero8, zero8, jnp.zeros((1, d_lru), F32), nb=1, tt=n_meta)

    h = x.reshape(bn * seq, d)
    h = _ffn(h, ffn1_pre_g, *w1, ffn1_post_g, tm=512, **ffn_tiles)
    h, _, _, _ = _mix(h, mp, xt, cvt, hl[0], nb=bn, tt=256)
    h = _ffn(h, ffn2_pre_g, *w2, ffn2_post_g, tm=512, **ffn_tiles)
    return h.reshape(bn, seq, d)
```

```python
import functools

import jax
import jax.numpy as jnp
from jax import lax
from jax.experimental import pallas as pl
from jax.experimental.pallas import tpu as pltpu

EPS = 1e-6
N_LRU_HEADS = 16
N_SCONV_GROUPS = 16
HEAD_DIM = 64
LRU_C = 8.0
SUBLANE = 8
MXU_DIM = 256
VMEM_LIMIT = 56 * 1024 * 1024
N_Z = 5

F32 = jnp.float32
BF16 = jnp.bfloat16


def _sigmoid(x):
    return 0.5 + 0.5 * jnp.tanh(0.5 * x)


def _rms(x, g):
    ms = jnp.mean(x * x, axis=-1, keepdims=True)
    return x * lax.rsqrt(ms + EPS) * g


def _ffn_step(h_ref, pre_g_ref, wg_ref, wu_ref, wd_ref, post_g_ref, o_ref, xn_ref, *, first, last, width, rc):
    tm = h_ref.shape[0]
    for r in range(tm // rc):
        rows = pl.ds(r * rc, rc)
        if first:
            xn = _rms(h_ref[rows, :], pre_g_ref[...]).astype(BF16)
            xn_ref[rows, :] = xn
        else:
            xn = xn_ref[rows, :]
        g = jnp.dot(xn, wg_ref[:, :width], preferred_element_type=F32)
        u = jnp.dot(xn, wu_ref[:, :width], preferred_element_type=F32)
        a = (g * _sigmoid(g) * u).astype(BF16)
        acc = jnp.dot(a, wd_ref[:width, :], preferred_element_type=F32)
        if not first:
            acc = o_ref[rows, :] + acc
        if last:
            acc = h_ref[rows, :] + 0.5 * _rms(acc, post_g_ref[...])
        o_ref[rows, :] = acc


def _ffn_kernel(h_ref, pre_g_ref, wg_ref, wu_ref, wd_ref, post_g_ref, o_ref, xn_ref, *, rem, rc):
    j = pl.program_id(1)
    nj = pl.num_programs(1)
    refs = (h_ref, pre_g_ref, wg_ref, wu_ref, wd_ref, post_g_ref, o_ref, xn_ref)
    tf = wg_ref.shape[1]

    @pl.when(j == 0)
    def _():
        _ffn_step(*refs, first=True, last=False, width=tf, rc=rc)

    @pl.when(jnp.logical_and(j > 0, j < nj - 1))
    def _():
        _ffn_step(*refs, first=False, last=False, width=tf, rc=rc)

    @pl.when(j == nj - 1)
    def _():
        _ffn_step(*refs, first=False, last=True, width=rem, rc=rc)


def _ffn(h, pre_g, wg, wu, wd, post_g, *, tm, tf, rc):
    m, d = h.shape
    f = wg.shape[1]
    nj = pl.cdiv(f, tf)
    rem = f - (nj - 1) * tf
    assert nj >= 2 and rem % 128 == 0 and m % tm == 0 and tm % rc == 0
    return pl.pallas_call(
        functools.partial(_ffn_kernel, rem=rem, rc=rc),
        grid=(m // tm, nj),
        in_specs=[
            pl.BlockSpec((tm, d), lambda i, j: (i, 0)),
            pl.BlockSpec((1, d), lambda i, j: (0, 0)),
            pl.BlockSpec((d, tf), lambda i, j: (0, j)),
            pl.BlockSpec((d, tf), lambda i, j: (0, j)),
            pl.BlockSpec((tf, d), lambda i, j: (j, 0)),
            pl.BlockSpec((1, d), lambda i, j: (0, 0)),
        ],
        out_specs=pl.BlockSpec((tm, d), lambda i, j: (i, 0)),
        out_shape=jax.ShapeDtypeStruct((m, d), F32),
        scratch_shapes=[pltpu.VMEM((tm, d), BF16)],
        compiler_params=pltpu.CompilerParams(
            dimension_semantics=("parallel", "arbitrary"), vmem_limit_bytes=VMEM_LIMIT),
        name="ffn",
    )(h, pre_g, wg, wu, wd, post_g)


def _shift_rows(x, k, head):
    r = pltpu.roll(x, k, 0)
    hr = pltpu.roll(head, k, 0)
    row = lax.broadcasted_iota(jnp.int32, head.shape, 0)
    first = jnp.where(row < k, hr, r[:SUBLANE])
    if x.shape[0] == SUBLANE:
        return first
    return jnp.concatenate([first, r[SUBLANE:]], axis=0)


def _shift_fill(x, s, fill):
    tt = x.shape[0]
    r = pltpu.roll(x, s, 0)
    row = lax.broadcasted_iota(jnp.int32, (SUBLANE,) + x.shape[1:], 0)
    first = jnp.where(row < s, fill, r[:SUBLANE])
    if tt == SUBLANE:
        return first
    return jnp.concatenate([first, r[SUBLANE:]], axis=0)


def _window_steps(a, u, steps):
    for s in steps:
        u = u + a * _shift_fill(u, s, 0.0)
        a = a * _shift_fill(a, s, 1.0)
    return a, u


def _scan_groups(a, u, h_in):
    hs = []
    h = h_in
    for g in range(a.shape[0] // SUBLANE):
        rows = slice(g * SUBLANE, (g + 1) * SUBLANE)
        h = a[rows] * h + u[rows]
        hs.append(h)
    return jnp.concatenate(hs, axis=0)


def _group_mean_sq(y, gmat):
    y2 = y * y
    hi = y2.astype(BF16)
    lo = (y2 - hi.astype(F32)).astype(BF16)
    s = jnp.dot(hi, gmat, preferred_element_type=F32) + jnp.dot(lo, gmat, preferred_element_type=F32)
    return s * (1.0 / HEAD_DIM)


def _gelu_tanh(x):
    return 0.5 * x * (1.0 + jnp.tanh(0.7978845608028654 * (x + 0.044715 * (x * x * x))))


def _mix_kernel(h_ref, pre_g_ref, win_ref, cw_ref, cb_ref, wax_ref, ba_ref, bx_ref, lam_ref, sw_ref, lg_ref,
                sg_ref, gmat_ref, wout_ref, post_g_ref, xh0_ref, cvh0_ref, h0_ref,
                o_ref, xt_ref, cvt_ref, hl_ref,
                xh_s, cvh_s, hc_s, un_s, z0_s, z1_s, m0_s, m1_s, acc_s, *, d_lru):
    @pl.when(pl.program_id(1) == 0)
    def _():
        xh_s[...] = xh0_ref[...]
        cvh_s[...] = cvh0_ref[...]
        hc_s[...] = h0_ref[...]

    tt, d = h_ref.shape
    w = MXU_DIM
    nc = d_lru // w
    z_slots = (z0_s, z1_s)
    m_slots = (m0_s, m1_s)
    gmat = gmat_ref[...]
    un_s[...] = _rms(h_ref[...], pre_g_ref[...]).astype(BF16)

    def project_pieces(c):
        def piece(k):
            def run():
                col = (c * N_Z + k) * w
                z_slots[c % 2][:, k * w:(k + 1) * w] = jnp.dot(
                    un_s[...], win_ref[:, col:col + w], preferred_element_type=F32)
            return run
        return [piece(k) for k in range(N_Z)]

    def outproj_pieces(c):
        half = d // 2

        def piece(n):
            def run():
                ncols = slice(n * half, (n + 1) * half)
                part = jnp.dot(m_slots[c % 2][...], wout_ref[c * 2 * w:(c + 1) * 2 * w, ncols],
                               preferred_element_type=F32)
                if c == 0:
                    acc_s[:, ncols] = part
                else:
                    acc_s[:, ncols] += part
            return run
        return [piece(0), piece(1)]

    def mixer(c):
        z = z_slots[c % 2]
        m = m_slots[c % 2]
        cols = slice(c * w, (c + 1) * w)
        x = z[:, w:2 * w]
        head = xh_s[:, cols]
        xc = (cb_ref[:, cols] + cw_ref[3:4, cols] * x
              + cw_ref[2:3, cols] * _shift_rows(x, 1, head)
              + cw_ref[1:2, cols] * _shift_rows(x, 2, head)
              + cw_ref[0:1, cols] * _shift_rows(x, 3, head))
        xh_s[:, cols] = x[tt - SUBLANE:]
        pre = jnp.dot(xc.astype(BF16), wax_ref[c], preferred_element_type=F32)
        yield
        cv = z[:, 3 * w:4 * w] * z[:, 4 * w:5 * w]
        chead = cvh_s[:, cols]
        sc = z[:, 2 * w:3 * w] * (sw_ref[2:3, cols] * cv
                                  + sw_ref[1:2, cols] * _shift_rows(cv, 1, chead)
                                  + sw_ref[0:1, cols] * _shift_rows(cv, 2, chead))
        cvh_s[:, cols] = cv[tt - SUBLANE:]
        sc_ms = _group_mean_sq(sc, gmat)
        yield
        ga = _sigmoid(pre[:, :w] + ba_ref[:, cols])
        gx = _sigmoid(pre[:, w:] + bx_ref[:, cols])
        nl = -lam_ref[:, cols]
        sp = jnp.maximum(nl, 0.0) + jnp.log1p(jnp.exp(-jnp.abs(nl)))
        log_a = (-LRU_C * ga) * sp
        a = jnp.exp(log_a)
        mult = jnp.sqrt(-jnp.tanh(log_a) * (a * a + 1.0))
        u = mult * (gx * xc)
        yield
        a, u = _window_steps(a, u, (1, 2))
        yield
        a, u = _window_steps(a, u, (4,))
        h = _scan_groups(a, u, hc_s[:, cols])
        hc_s[:, cols] = h[tt - 1:]
        yield
        lo = h * _gelu_tanh(z[:, 0:w])
        lo_ms = _group_mean_sq(lo, gmat)
        yield
        m[:, w:] = (sc * lax.rsqrt(sc_ms + EPS) * sg_ref[:, cols]).astype(BF16)
        m[:, :w] = (lo * lax.rsqrt(lo_ms + EPS) * lg_ref[:, cols]).astype(BF16)
        yield

    for run in project_pieces(0):
        run()
    pending = []
    for c in range(nc):
        if c + 1 < nc:
            pending += project_pieces(c + 1)
        for _ in mixer(c):
            if pending:
                pending.pop(0)()
        while pending:
            pending.pop(0)()
        pending = outproj_pieces(c)
    for run in pending:
        run()

    o_ref[...] = h_ref[...] + _rms(acc_s[...], post_g_ref[...])
    xt_ref[...] = xh_s[...]
    cvt_ref[...] = cvh_s[...]
    hl_ref[0] = hc_s[...]


def _mix(h, p, xh0, cvh0, h0, *, nb, tt):
    rows, d = h.shape
    d_lru = p["cb"].shape[1]
    nt = rows // (nb * tt)
    assert rows == nb * nt * tt
    const = lambda b, t: (0, 0)
    once = pl.Buffered(1)
    vec = pl.BlockSpec((1, d_lru), const)
    gvec = pl.BlockSpec((1, d), const)
    state8 = jax.ShapeDtypeStruct((nb * SUBLANE, d_lru), F32)
    return pl.pallas_call(
        functools.partial(_mix_kernel, d_lru=d_lru),
        grid=(nb, nt),
        in_specs=[
            pl.BlockSpec((tt, d), lambda b, t: (b * nt + t, 0)),
            gvec,
            pl.BlockSpec(p["win"].shape, const, pipeline_mode=once),
            pl.BlockSpec(p["cw"].shape, const), vec,
            pl.BlockSpec(p["wax"].shape, lambda b, t: (0, 0, 0)),
            vec, vec, vec,
            pl.BlockSpec(p["sw"].shape, const), vec, vec,
            pl.BlockSpec((MXU_DIM, MXU_DIM), const),
            pl.BlockSpec(p["wout"].shape, const, pipeline_mode=once),
            gvec,
            pl.BlockSpec((SUBLANE, d_lru), const),
            pl.BlockSpec((SUBLANE, d_lru), const),
            vec,
        ],
        out_specs=[
            pl.BlockSpec((tt, d), lambda b, t: (b * nt + t, 0)),
            pl.BlockSpec((SUBLANE, d_lru), lambda b, t: (b, 0)),
            pl.BlockSpec((SUBLANE, d_lru), lambda b, t: (b, 0)),
            pl.BlockSpec((1, 1, d_lru), lambda b, t: (b, 0, 0)),
        ],
        out_shape=[
            jax.ShapeDtypeStruct((rows, d), F32),
            state8, state8,
            jax.ShapeDtypeStruct((nb, 1, d_lru), F32),
        ],
        scratch_shapes=[
            pltpu.VMEM((SUBLANE, d_lru), F32),
            pltpu.VMEM((SUBLANE, d_lru), F32),
            pltpu.VMEM((1, d_lru), F32),
            pltpu.VMEM((tt, d), BF16),
            pltpu.VMEM((tt, N_Z * MXU_DIM), F32),
            pltpu.VMEM((tt, N_Z * MXU_DIM), F32),
            pltpu.VMEM((tt, 2 * MXU_DIM), BF16),
            pltpu.VMEM((tt, 2 * MXU_DIM), BF16),
            pltpu.VMEM((tt, d), F32),
        ],
        compiler_params=pltpu.CompilerParams(
            dimension_semantics=("parallel", "arbitrary"), vmem_limit_bytes=VMEM_LIMIT),
        name="mix",
    )(h, p["pre_g"], p["win"], p["cw"], p["cb"], p["wax"], p["ba"], p["bx"], p["lam"], p["sw"], p["lg"],
      p["sg"], p["gmat"], p["wout"], p["post_g"], xh0, cvh0, h0)


def _block_diag_gates(w_a, w_x):
    nh, blk, _ = w_a.shape
    per = MXU_DIM // blk
    eye = jnp.eye(per, dtype=w_a.dtype)

    def bd(w):
        w = w.reshape(nh // per, per, blk, blk)
        return jnp.einsum("cpij,pq->cpiqj", w, eye).reshape(nh // per, MXU_DIM, MXU_DIM)

    return jnp.concatenate([bd(w_a), bd(w_x)], axis=-1).astype(BF16)


def kernel(x, meta_tokens, ffn1_pre_g, ffn1_w_gate, ffn1_w_up, ffn1_w_down, ffn1_post_g, mix_pre_g, w_in,
           lru_conv_w, lru_conv_b, lru_w_a, lru_b_a, lru_w_x, lru_b_x, lru_lambda, sconv_w, lru_out_g,
           sconv_out_g, w_out, mix_post_g, ffn2_pre_g, ffn2_w_gate, ffn2_w_up, ffn2_w_down, ffn2_post_g):
    bn, seq, d = x.shape
    n_meta = meta_tokens.shape[0]
    d_lru = lru_conv_w.shape[-1]
    d_sc = sconv_w.shape[-1]
    assert ffn1_pre_g.shape[0] == 1, "single layer"
    assert n_meta % SUBLANE == 0 and n_meta >= SUBLANE
    assert d_lru == d_sc == N_LRU_HEADS * HEAD_DIM == N_SCONV_GROUPS * HEAD_DIM
    assert lru_w_a.shape[-1] == HEAD_DIM and d_lru % MXU_DIM == 0
    assert w_in.shape[-1] == N_Z * d_lru and w_out.shape[1] == 2 * d_lru

    w1 = tuple(w[0].astype(BF16) for w in (ffn1_w_gate, ffn1_w_up, ffn1_w_down))
    w2 = tuple(w[0].astype(BF16) for w in (ffn2_w_gate, ffn2_w_up, ffn2_w_down))
    nc = d_lru // MXU_DIM
    head = jnp.arange(MXU_DIM) // HEAD_DIM
    mp = {
        "pre_g": mix_pre_g,
        "win": w_in[0].reshape(d, N_Z, nc, MXU_DIM).transpose(0, 2, 1, 3).reshape(d, N_Z * d_lru).astype(BF16),
        "cw": lru_conv_w[0], "cb": lru_conv_b,
        "wax": _block_diag_gates(lru_w_a[0], lru_w_x[0]), "ba": lru_b_a, "bx": lru_b_x, "lam": lru_lambda,
        "sw": sconv_w[0], "lg": lru_out_g, "sg": sconv_out_g,
        "gmat": (head[:, None] == head[None, :]).astype(BF16),
        "wout": w_out[0].reshape(2, nc, MXU_DIM, d).transpose(1, 0, 2, 3).reshape(2 * d_lru, d).astype(BF16),
        "post_g": mix_post_g,
    }
    ffn_tiles = dict(tf=512, rc=256)

    hm = _ffn(meta_tokens, ffn1_pre_g, *w1, ffn1_post_g, tm=n_meta, rc=n_meta, tf=ffn_tiles["tf"])
    zero8 = jnp.zeros((SUBLANE, d_lru), F32)
    _, xt, cvt, hl = _mix(hm, mp, zero8, zero8, jnp.zeros((1, d_lru), F32), nb=1, tt=n_meta)

    h = x.reshape(bn * seq, d)
    h = _ffn(h, ffn1_pre_g, *w1, ffn1_post_g, tm=1024, **ffn_tiles)
    h, _, _, _ = _mix(h, mp, xt, cvt, hl[0], nb=bn, tt=256)
    h = _ffn(h, ffn2_pre_g, *w2, ffn2_post_g, tm=1024, **ffn_tiles)
    return h.reshape(bn, seq, d)
```

```python
import functools

import jax
import jax.numpy as jnp
from jax import lax
from jax.experimental import pallas as pl
from jax.experimental.pallas import tpu as pltpu

EPS = 1e-6
N_LRU_HEADS = 16
N_SCONV_GROUPS = 16
HEAD_DIM = 64
LRU_C = 8.0
SUBLANE = 8
MXU_DIM = 256
VMEM_LIMIT = 56 * 1024 * 1024
N_Z = 5

F32 = jnp.float32
BF16 = jnp.bfloat16


def _sigmoid(x):
    return 0.5 + 0.5 * jnp.tanh(0.5 * x)


def _rms(x, g):
    ms = jnp.mean(x * x, axis=-1, keepdims=True)
    return x * lax.rsqrt(ms + EPS) * g


def _ffn_step(h_ref, pre_g_ref, wg_ref, wu_ref, wd_ref, post_g_ref, o_ref, xn_ref, *, first, last, width, rc, sig):
    tm = h_ref.shape[0]
    for r in range(tm // rc):
        rows = pl.ds(r * rc, rc)
        if first:
            xn = _rms(h_ref[rows, :], pre_g_ref[...]).astype(BF16)
            xn_ref[rows, :] = xn
        else:
            xn = xn_ref[rows, :]
        g = jnp.dot(xn, wg_ref[:, :width], preferred_element_type=F32)
        u = jnp.dot(xn, wu_ref[:, :width], preferred_element_type=F32)
        a = (g * sig(g) * u).astype(BF16)
        acc = jnp.dot(a, wd_ref[:width, :], preferred_element_type=F32)
        if not first:
            acc = o_ref[rows, :] + acc
        if last:
            acc = h_ref[rows, :] + 0.5 * _rms(acc, post_g_ref[...])
        o_ref[rows, :] = acc


def _ffn_kernel(h_ref, pre_g_ref, wg_ref, wu_ref, wd_ref, post_g_ref, o_ref, xn_ref, *, rem, rc, sig):
    j = pl.program_id(1)
    nj = pl.num_programs(1)
    refs = (h_ref, pre_g_ref, wg_ref, wu_ref, wd_ref, post_g_ref, o_ref, xn_ref)
    tf = wg_ref.shape[1]

    @pl.when(j == 0)
    def _():
        _ffn_step(*refs, first=True, last=False, width=tf, rc=rc, sig=sig)

    @pl.when(jnp.logical_and(j > 0, j < nj - 1))
    def _():
        _ffn_step(*refs, first=False, last=False, width=tf, rc=rc, sig=sig)

    @pl.when(j == nj - 1)
    def _():
        _ffn_step(*refs, first=False, last=True, width=rem, rc=rc, sig=sig)


def _ffn(h, pre_g, wg, wu, wd, post_g, *, tm, tf, rc, sig=jax.nn.sigmoid):
    m, d = h.shape
    f = wg.shape[1]
    nj = pl.cdiv(f, tf)
    rem = f - (nj - 1) * tf
    assert nj >= 2 and rem % 128 == 0 and m % tm == 0 and tm % rc == 0
    return pl.pallas_call(
        functools.partial(_ffn_kernel, rem=rem, rc=rc, sig=sig),
        grid=(m // tm, nj),
        in_specs=[
            pl.BlockSpec((tm, d), lambda i, j: (i, 0)),
            pl.BlockSpec((1, d), lambda i, j: (0, 0)),
            pl.BlockSpec((d, tf), lambda i, j: (0, j)),
            pl.BlockSpec((d, tf), lambda i, j: (0, j)),
            pl.BlockSpec((tf, d), lambda i, j: (j, 0)),
            pl.BlockSpec((1, d), lambda i, j: (0, 0)),
        ],
        out_specs=pl.BlockSpec((tm, d), lambda i, j: (i, 0)),
        out_shape=jax.ShapeDtypeStruct((m, d), F32),
        scratch_shapes=[pltpu.VMEM((tm, d), BF16)],
        compiler_params=pltpu.CompilerParams(
            dimension_semantics=("parallel", "arbitrary"), vmem_limit_bytes=VMEM_LIMIT),
        name="ffn",
    )(h, pre_g, wg, wu, wd, post_g)


def _shift_rows(x, k, head):
    r = pltpu.roll(x, k, 0)
    hr = pltpu.roll(head, k, 0)
    row = lax.broadcasted_iota(jnp.int32, head.shape, 0)
    first = jnp.where(row < k, hr, r[:SUBLANE])
    if x.shape[0] == SUBLANE:
        return first
    return jnp.concatenate([first, r[SUBLANE:]], axis=0)


def _shift_fill(x, s, fill):
    tt = x.shape[0]
    r = pltpu.roll(x, s, 0)
    row = lax.broadcasted_iota(jnp.int32, (SUBLANE,) + x.shape[1:], 0)
    first = jnp.where(row < s, fill, r[:SUBLANE])
    if tt == SUBLANE:
        return first
    return jnp.concatenate([first, r[SUBLANE:]], axis=0)


def _window_steps(a, u, steps):
    for s in steps:
        u = u + a * _shift_fill(u, s, 0.0)
        a = a * _shift_fill(a, s, 1.0)
    return a, u


def _scan_groups(a, u, h_in):
    hs = []
    h = h_in
    for g in range(a.shape[0] // SUBLANE):
        rows = slice(g * SUBLANE, (g + 1) * SUBLANE)
        h = a[rows] * h + u[rows]
        hs.append(h)
    return jnp.concatenate(hs, axis=0)


def _group_mean_sq(y, gmat):
    y2 = y * y
    hi = y2.astype(BF16)
    lo = (y2 - hi.astype(F32)).astype(BF16)
    s = jnp.dot(hi, gmat, preferred_element_type=F32) + jnp.dot(lo, gmat, preferred_element_type=F32)
    return s * (1.0 / HEAD_DIM)


def _gelu_tanh(x):
    return 0.5 * x * (1.0 + jnp.tanh(0.7978845608028654 * (x + 0.044715 * (x * x * x))))


def _mix_kernel(h_ref, pre_g_ref, win_ref, cw_ref, cb_ref, wax_ref, ba_ref, bx_ref, lam_ref, sw_ref, lg_ref,
                sg_ref, gmat_ref, wout_ref, post_g_ref, xh0_ref, cvh0_ref, h0_ref,
                o_ref, xt_ref, cvt_ref, hl_ref,
                xh_s, cvh_s, hc_s, un_s, z0_s, z1_s, m0_s, m1_s, acc_s, *, d_lru):
    @pl.when(pl.program_id(1) == 0)
    def _():
        xh_s[...] = xh0_ref[...]
        cvh_s[...] = cvh0_ref[...]
        hc_s[...] = h0_ref[...]

    tt, d = h_ref.shape
    w = MXU_DIM
    nc = d_lru // w
    z_slots = (z0_s, z1_s)
    m_slots = (m0_s, m1_s)
    gmat = gmat_ref[...]
    un_s[...] = _rms(h_ref[...], pre_g_ref[...]).astype(BF16)

    def project_pieces(c):
        def piece(k):
            def run():
                col = (c * N_Z + k) * w
                z_slots[c % 2][:, k * w:(k + 1) * w] = jnp.dot(
                    un_s[...], win_ref[:, col:col + w], preferred_element_type=F32)
            return run
        return [piece(k) for k in range(N_Z)]

    def outproj_pieces(c):
        half = d // 2

        def piece(n):
            def run():
                ncols = slice(n * half, (n + 1) * half)
                part = jnp.dot(m_slots[c % 2][...], wout_ref[c * 2 * w:(c + 1) * 2 * w, ncols],
                               preferred_element_type=F32)
                if c == 0:
                    acc_s[:, ncols] = part
                else:
                    acc_s[:, ncols] += part
            return run
        return [piece(0), piece(1)]

    def mixer(c):
        z = z_slots[c % 2]
        m = m_slots[c % 2]
        cols = slice(c * w, (c + 1) * w)
        x = z[:, w:2 * w]
        head = xh_s[:, cols]
        xc = (cb_ref[:, cols] + cw_ref[3:4, cols] * x
              + cw_ref[2:3, cols] * _shift_rows(x, 1, head)
              + cw_ref[1:2, cols] * _shift_rows(x, 2, head)
              + cw_ref[0:1, cols] * _shift_rows(x, 3, head))
        xh_s[:, cols] = x[tt - SUBLANE:]
        pre = jnp.dot(xc.astype(BF16), wax_ref[c], preferred_element_type=F32)
        yield
        cv = z[:, 3 * w:4 * w] * z[:, 4 * w:5 * w]
        chead = cvh_s[:, cols]
        sc = z[:, 2 * w:3 * w] * (sw_ref[2:3, cols] * cv
                                  + sw_ref[1:2, cols] * _shift_rows(cv, 1, chead)
                                  + sw_ref[0:1, cols] * _shift_rows(cv, 2, chead))
        cvh_s[:, cols] = cv[tt - SUBLANE:]
        sc_ms = _group_mean_sq(sc, gmat)
        yield
        ga = _sigmoid(pre[:, :w] + ba_ref[:, cols])
        gx = _sigmoid(pre[:, w:] + bx_ref[:, cols])
        nl = -lam_ref[:, cols]
        sp = jnp.maximum(nl, 0.0) + jnp.log1p(jnp.exp(-jnp.abs(nl)))
        log_a = (-LRU_C * ga) * sp
        a = jnp.exp(log_a)
        mult = jnp.sqrt(-jnp.tanh(log_a) * (a * a + 1.0))
        u = mult * (gx * xc)
        yield
        a, u = _window_steps(a, u, (1, 2))
        yield
        a, u = _window_steps(a, u, (4,))
        h = _scan_groups(a, u, hc_s[:, cols])
        hc_s[:, cols] = h[tt - 1:]
        yield
        lo = h * _gelu_tanh(z[:, 0:w])
        lo_ms = _group_mean_sq(lo, gmat)
        yield
        m[:, w:] = (sc * lax.rsqrt(sc_ms + EPS) * sg_ref[:, cols]).astype(BF16)
        m[:, :w] = (lo * lax.rsqrt(lo_ms + EPS) * lg_ref[:, cols]).astype(BF16)
        yield

    for run in project_pieces(0):
        run()
    pending = []
    for c in range(nc):
        if c + 1 < nc:
            pending += project_pieces(c + 1)
        for _ in mixer(c):
            if pending:
                pending.pop(0)()
        while pending:
            pending.pop(0)()
        pending = outproj_pieces(c)
    for run in pending:
        run()

    o_ref[...] = h_ref[...] + _rms(acc_s[...], post_g_ref[...])
    xt_ref[...] = xh_s[...]
    cvt_ref[...] = cvh_s[...]
    hl_ref[0] = hc_s[...]


def _mix(h, p, xh0, cvh0, h0, *, nb, tt):
    rows, d = h.shape
    d_lru = p["cb"].shape[1]
    nt = rows // (nb * tt)
    assert rows == nb * nt * tt
    const = lambda b, t: (0, 0)
    once = pl.Buffered(1)
    vec = pl.BlockSpec((1, d_lru), const)
    gvec = pl.BlockSpec((1, d), const)
    state8 = jax.ShapeDtypeStruct((nb * SUBLANE, d_lru), F32)
    return pl.pallas_call(
        functools.partial(_mix_kernel, d_lru=d_lru),
        grid=(nb, nt),
        in_specs=[
            pl.BlockSpec((tt, d), lambda b, t: (b * nt + t, 0)),
            gvec,
            pl.BlockSpec(p["win"].shape, const, pipeline_mode=once),
            pl.BlockSpec(p["cw"].shape, const), vec,
            pl.BlockSpec(p["wax"].shape, lambda b, t: (0, 0, 0)),
            vec, vec, vec,
            pl.BlockSpec(p["sw"].shape, const), vec, vec,
            pl.BlockSpec((MXU_DIM, MXU_DIM), const),
            pl.BlockSpec(p["wout"].shape, const, pipeline_mode=once),
            gvec,
            pl.BlockSpec((SUBLANE, d_lru), const),
            pl.BlockSpec((SUBLANE, d_lru), const),
            vec,
        ],
        out_specs=[
            pl.BlockSpec((tt, d), lambda b, t: (b * nt + t, 0)),
            pl.BlockSpec((SUBLANE, d_lru), lambda b, t: (b, 0)),
            pl.BlockSpec((SUBLANE, d_lru), lambda b, t: (b, 0)),
            pl.BlockSpec((1, 1, d_lru), lambda b, t: (b, 0, 0)),
        ],
        out_shape=[
            jax.ShapeDtypeStruct((rows, d), F32),
            state8, state8,
            jax.ShapeDtypeStruct((nb, 1, d_lru), F32),
        ],
        scratch_shapes=[
            pltpu.VMEM((SUBLANE, d_lru), F32),
            pltpu.VMEM((SUBLANE, d_lru), F32),
            pltpu.VMEM((1, d_lru), F32),
            pltpu.VMEM((tt, d), BF16),
            pltpu.VMEM((tt, N_Z * MXU_DIM), F32),
            pltpu.VMEM((tt, N_Z * MXU_DIM), F32),
            pltpu.VMEM((tt, 2 * MXU_DIM), BF16),
            pltpu.VMEM((tt, 2 * MXU_DIM), BF16),
            pltpu.VMEM((tt, d), F32),
        ],
        compiler_params=pltpu.CompilerParams(
            dimension_semantics=("parallel", "arbitrary"), vmem_limit_bytes=VMEM_LIMIT),
        name="mix",
    )(h, p["pre_g"], p["win"], p["cw"], p["cb"], p["wax"], p["ba"], p["bx"], p["lam"], p["sw"], p["lg"],
      p["sg"], p["gmat"], p["wout"], p["post_g"], xh0, cvh0, h0)


def _block_diag_gates(w_a, w_x):
    nh, blk, _ = w_a.shape
    per = MXU_DIM // blk
    eye = jnp.eye(per, dtype=w_a.dtype)

    def bd(w):
        w = w.reshape(nh // per, per, blk, blk)
        return jnp.einsum("cpij,pq->cpiqj", w, eye).reshape(nh // per, MXU_DIM, MXU_DIM)

    return jnp.concatenate([bd(w_a), bd(w_x)], axis=-1).astype(BF16)


def kernel(x, meta_tokens, ffn1_pre_g, ffn1_w_gate, ffn1_w_up, ffn1_w_down, ffn1_post_g, mix_pre_g, w_in,
           lru_conv_w, lru_conv_b, lru_w_a, lru_b_a, lru_w_x, lru_b_x, lru_lambda, sconv_w, lru_out_g,
           sconv_out_g, w_out, mix_post_g, ffn2_pre_g, ffn2_w_gate, ffn2_w_up, ffn2_w_down, ffn2_post_g):
    bn, seq, d = x.shape
    n_meta = meta_tokens.shape[0]
    d_lru = lru_conv_w.shape[-1]
    d_sc = sconv_w.shape[-1]
    assert ffn1_pre_g.shape[0] == 1, "single layer"
    assert n_meta % SUBLANE == 0 and n_meta >= SUBLANE
    assert d_lru == d_sc == N_LRU_HEADS * HEAD_DIM == N_SCONV_GROUPS * HEAD_DIM
    assert lru_w_a.shape[-1] == HEAD_DIM and d_lru % MXU_DIM == 0
    assert w_in.shape[-1] == N_Z * d_lru and w_out.shape[1] == 2 * d_lru

    w1 = tuple(w[0].astype(BF16) for w in (ffn1_w_gate, ffn1_w_up, ffn1_w_down))
    w2 = tuple(w[0].astype(BF16) for w in (ffn2_w_gate, ffn2_w_up, ffn2_w_down))
    nc = d_lru // MXU_DIM
    head = jnp.arange(MXU_DIM) // HEAD_DIM
    mp = {
        "pre_g": mix_pre_g,
        "win": w_in[0].reshape(d, N_Z, nc, MXU_DIM).transpose(0, 2, 1, 3).reshape(d, N_Z * d_lru).astype(BF16),
        "cw": lru_conv_w[0], "cb": lru_conv_b,
        "wax": _block_diag_gates(lru_w_a[0], lru_w_x[0]), "ba": lru_b_a, "bx": lru_b_x, "lam": lru_lambda,
        "sw": sconv_w[0], "lg": lru_out_g, "sg": sconv_out_g,
        "gmat": (head[:, None] == head[None, :]).astype(BF16),
        "wout": w_out[0].reshape(2, nc, MXU_DIM, d).transpose(1, 0, 2, 3).reshape(2 * d_lru, d).astype(BF16),
        "post_g": mix_post_g,
    }
    ffn_tiles = dict(tf=512, rc=256)

    hm = _ffn(meta_tokens, ffn1_pre_g, *w1, ffn1_post_g, tm=n_meta, rc=n_meta, tf=ffn_tiles["tf"])
    zero8 = jnp.zeros((SUBLANE, d_lru), F32)
    _, xt, cvt, hl = _mix(hm, mp, zero8, zero8, jnp.zeros((1, d_lru), F32), nb=1, tt=n_meta)

    h = x.reshape(bn * seq, d)
    h = _ffn(h, ffn1_pre_g, *w1, ffn1_post_g, tm=1024, **ffn_tiles)
    h, _, _, _ = _mix(h, mp, xt, cvt, hl[0], nb=bn, tt=256)
    h = _ffn(h, ffn2_pre_g, *w2, ffn2_post_g, tm=1024, sig=_sigmoid, **ffn_tiles)
    return h.reshape(bn, seq, d)
```

```python
import functools

import jax
import jax.numpy as jnp
from jax import lax
from jax.experimental import pallas as pl
from jax.experimental.pallas import tpu as pltpu

EPS = 1e-6
N_LRU_HEADS = 16
N_SCONV_GROUPS = 16
HEAD_DIM = 64
LRU_C = 8.0
SUBLANE = 8
MXU_DIM = 256
VMEM_LIMIT = 56 * 1024 * 1024
N_Z = 5

F32 = jnp.float32
BF16 = jnp.bfloat16


def _sigmoid(x):
    return 0.5 + 0.5 * jnp.tanh(0.5 * x)


def _rms(x, g):
    ms = jnp.mean(x * x, axis=-1, keepdims=True)
    return x * lax.rsqrt(ms + EPS) * g


def _ffn_step(h_ref, pre_g_ref, wg_ref, wu_ref, wd_ref, post_g_ref, o_ref, xn_ref, *, first, last, width, rc):
    tm = h_ref.shape[0]
    for r in range(tm // rc):
        rows = pl.ds(r * rc, rc)
        if first:
            xn = _rms(h_ref[rows, :], pre_g_ref[...]).astype(BF16)
            xn_ref[rows, :] = xn
        else:
            xn = xn_ref[rows, :]
        g = jnp.dot(xn, wg_ref[:, :width], preferred_element_type=F32)
        u = jnp.dot(xn, wu_ref[:, :width], preferred_element_type=F32)
        a = (g * _sigmoid(g) * u).astype(BF16)
        acc = jnp.dot(a, wd_ref[:width, :], preferred_element_type=F32)
        if not first:
            acc = o_ref[rows, :] + acc
        if last:
            acc = h_ref[rows, :] + 0.5 * _rms(acc, post_g_ref[...])
        o_ref[rows, :] = acc


def _ffn_kernel(h_ref, pre_g_ref, wg_ref, wu_ref, wd_ref, post_g_ref, o_ref, xn_ref, *, rem, rc_edge):
    j = pl.program_id(1)
    nj = pl.num_programs(1)
    refs = (h_ref, pre_g_ref, wg_ref, wu_ref, wd_ref, post_g_ref, o_ref, xn_ref)
    tm = h_ref.shape[0]
    tf = wg_ref.shape[1]

    @pl.when(j == 0)
    def _():
        _ffn_step(*refs, first=True, last=False, width=tf, rc=rc_edge)

    @pl.when(jnp.logical_and(j > 0, j < nj - 1))
    def _():
        _ffn_step(*refs, first=False, last=False, width=tf, rc=tm)

    @pl.when(j == nj - 1)
    def _():
        _ffn_step(*refs, first=False, last=True, width=rem, rc=rc_edge)


def _ffn(h, pre_g, wg, wu, wd, post_g, *, tm, tf, rc_edge):
    m, d = h.shape
    f = wg.shape[1]
    nj = pl.cdiv(f, tf)
    rem = f - (nj - 1) * tf
    assert nj >= 2 and rem % 128 == 0 and m % tm == 0 and tm % rc_edge == 0
    return pl.pallas_call(
        functools.partial(_ffn_kernel, rem=rem, rc_edge=rc_edge),
        grid=(m // tm, nj),
        in_specs=[
            pl.BlockSpec((tm, d), lambda i, j: (i, 0)),
            pl.BlockSpec((1, d), lambda i, j: (0, 0)),
            pl.BlockSpec((d, tf), lambda i, j: (0, j)),
            pl.BlockSpec((d, tf), lambda i, j: (0, j)),
            pl.BlockSpec((tf, d), lambda i, j: (j, 0)),
            pl.BlockSpec((1, d), lambda i, j: (0, 0)),
        ],
        out_specs=pl.BlockSpec((tm, d), lambda i, j: (i, 0)),
        out_shape=jax.ShapeDtypeStruct((m, d), F32),
        scratch_shapes=[pltpu.VMEM((tm, d), BF16)],
        compiler_params=pltpu.CompilerParams(
            dimension_semantics=("parallel", "arbitrary"), vmem_limit_bytes=VMEM_LIMIT),
        name="ffn",
    )(h, pre_g, wg, wu, wd, post_g)


def _shift_rows(x, k, head):
    r = pltpu.roll(x, k, 0)
    hr = pltpu.roll(head, k, 0)
    row = lax.broadcasted_iota(jnp.int32, head.shape, 0)
    first = jnp.where(row < k, hr, r[:SUBLANE])
    if x.shape[0] == SUBLANE:
        return first
    return jnp.concatenate([first, r[SUBLANE:]], axis=0)


def _shift_fill(x, s, fill):
    tt = x.shape[0]
    r = pltpu.roll(x, s, 0)
    row = lax.broadcasted_iota(jnp.int32, (SUBLANE,) + x.shape[1:], 0)
    first = jnp.where(row < s, fill, r[:SUBLANE])
    if tt == SUBLANE:
        return first
    return jnp.concatenate([first, r[SUBLANE:]], axis=0)


def _window_steps(a, u, steps):
    for s in steps:
        u = u + a * _shift_fill(u, s, 0.0)
        a = a * _shift_fill(a, s, 1.0)
    return a, u


def _scan_groups(a, u, h_in):
    hs = []
    h = h_in
    for g in range(a.shape[0] // SUBLANE):
        rows = slice(g * SUBLANE, (g + 1) * SUBLANE)
        h = a[rows] * h + u[rows]
        hs.append(h)
    return jnp.concatenate(hs, axis=0)


def _group_mean_sq(y, gmat):
    y2 = y * y
    hi = y2.astype(BF16)
    lo = (y2 - hi.astype(F32)).astype(BF16)
    s = jnp.dot(hi, gmat, preferred_element_type=F32) + jnp.dot(lo, gmat, preferred_element_type=F32)
    return s * (1.0 / HEAD_DIM)


def _gelu_tanh(x):
    return 0.5 * x * (1.0 + jnp.tanh(0.7978845608028654 * (x + 0.044715 * (x * x * x))))


def _mix_kernel(h_ref, pre_g_ref, win_ref, cw_ref, cb_ref, wax_ref, ba_ref, bx_ref, lam_ref, sw_ref, lg_ref,
                sg_ref, gmat_ref, wout_ref, post_g_ref, xh0_ref, cvh0_ref, h0_ref,
                o_ref, xt_ref, cvt_ref, hl_ref,
                xh_s, cvh_s, hc_s, un_s, z0_s, z1_s, m0_s, m1_s, *, d_lru):
    @pl.when(pl.program_id(1) == 0)
    def _():
        xh_s[...] = xh0_ref[...]
        cvh_s[...] = cvh0_ref[...]
        hc_s[...] = h0_ref[...]

    tt, d = h_ref.shape
    w = MXU_DIM
    nc = d_lru // w
    z_slots = (z0_s, z1_s)
    m_slots = (m0_s, m1_s)
    gmat = gmat_ref[...]
    un_s[...] = _rms(h_ref[...], pre_g_ref[...]).astype(BF16)

    def project_pieces(c):
        def piece(k):
            def run():
                col = (c * N_Z + k) * w
                z_slots[c % 2][:, k * w:(k + 1) * w] = jnp.dot(
                    un_s[...], win_ref[:, col:col + w], preferred_element_type=F32)
            return run
        return [piece(k) for k in range(N_Z)]

    def outproj_pieces(c):
        half = d // 2

        def piece(n):
            def run():
                ncols = slice(n * half, (n + 1) * half)
                part = jnp.dot(m_slots[c % 2][...], wout_ref[c * 2 * w:(c + 1) * 2 * w, ncols],
                               preferred_element_type=F32)
                if c == 0:
                    o_ref[:, ncols] = part
                else:
                    o_ref[:, ncols] += part
            return run
        return [piece(0), piece(1)]

    def mixer(c):
        z = z_slots[c % 2]
        m = m_slots[c % 2]
        cols = slice(c * w, (c + 1) * w)
        x = z[:, w:2 * w]
        head = xh_s[:, cols]
        xc = (cb_ref[:, cols] + cw_ref[3:4, cols] * x
              + cw_ref[2:3, cols] * _shift_rows(x, 1, head)
              + cw_ref[1:2, cols] * _shift_rows(x, 2, head)
              + cw_ref[0:1, cols] * _shift_rows(x, 3, head))
        xh_s[:, cols] = x[tt - SUBLANE:]
        pre = jnp.dot(xc.astype(BF16), wax_ref[c], preferred_element_type=F32)
        yield
        cv = z[:, 3 * w:4 * w] * z[:, 4 * w:5 * w]
        chead = cvh_s[:, cols]
        sc = z[:, 2 * w:3 * w] * (sw_ref[2:3, cols] * cv
                                  + sw_ref[1:2, cols] * _shift_rows(cv, 1, chead)
                                  + sw_ref[0:1, cols] * _shift_rows(cv, 2, chead))
        cvh_s[:, cols] = cv[tt - SUBLANE:]
        sc_ms = _group_mean_sq(sc, gmat)
        yield
        ga = _sigmoid(pre[:, :w] + ba_ref[:, cols])
        gx = _sigmoid(pre[:, w:] + bx_ref[:, cols])
        nl = -lam_ref[:, cols]
        sp = jnp.maximum(nl, 0.0) + jnp.log1p(jnp.exp(-jnp.abs(nl)))
        log_a = (-LRU_C * ga) * sp
        a = jnp.exp(log_a)
        mult = jnp.sqrt(-jnp.tanh(log_a) * (a * a + 1.0))
        u = mult * (gx * xc)
        yield
        a, u = _window_steps(a, u, (1, 2))
        yield
        a, u = _window_steps(a, u, (4,))
        h = _scan_groups(a, u, hc_s[:, cols])
        hc_s[:, cols] = h[tt - 1:]
        yield
        lo = h * _gelu_tanh(z[:, 0:w])
        lo_ms = _group_mean_sq(lo, gmat)
        yield
        m[:, w:] = (sc * lax.rsqrt(sc_ms + EPS) * sg_ref[:, cols]).astype(BF16)
        m[:, :w] = (lo * lax.rsqrt(lo_ms + EPS) * lg_ref[:, cols]).astype(BF16)
        yield

    for run in project_pieces(0):
        run()
    pending = []
    for c in range(nc):
        if c + 1 < nc:
            pending += project_pieces(c + 1)
        for _ in mixer(c):
            if pending:
                pending.pop(0)()
        while pending:
            pending.pop(0)()
        pending = outproj_pieces(c)
    for run in pending:
        run()

    o_ref[...] = h_ref[...] + _rms(o_ref[...], post_g_ref[...])
    xt_ref[...] = xh_s[...]
    cvt_ref[...] = cvh_s[...]
    hl_ref[0] = hc_s[...]


def _mix(h, p, xh0, cvh0, h0, *, nb, tt):
    rows, d = h.shape
    d_lru = p["cb"].shape[1]
    nt = rows // (nb * tt)
    assert rows == nb * nt * tt
    const = lambda b, t: (0, 0)
    once = pl.Buffered(1)
    vec = pl.BlockSpec((1, d_lru), const)
    gvec = pl.BlockSpec((1, d), const)
    state8 = jax.ShapeDtypeStruct((nb * SUBLANE, d_lru), F32)
    return pl.pallas_call(
        functools.partial(_mix_kernel, d_lru=d_lru),
        grid=(nb, nt),
        in_specs=[
            pl.BlockSpec((tt, d), lambda b, t: (b * nt + t, 0)),
            gvec,
            pl.BlockSpec(p["win"].shape, const, pipeline_mode=once),
            pl.BlockSpec(p["cw"].shape, const), vec,
            pl.BlockSpec(p["wax"].shape, lambda b, t: (0, 0, 0)),
            vec, vec, vec,
            pl.BlockSpec(p["sw"].shape, const), vec, vec,
            pl.BlockSpec((MXU_DIM, MXU_DIM), const),
            pl.BlockSpec(p["wout"].shape, const, pipeline_mode=once),
            gvec,
            pl.BlockSpec((SUBLANE, d_lru), const),
            pl.BlockSpec((SUBLANE, d_lru), const),
            vec,
        ],
        out_specs=[
            pl.BlockSpec((tt, d), lambda b, t: (b * nt + t, 0)),
            pl.BlockSpec((SUBLANE, d_lru), lambda b, t: (b, 0)),
            pl.BlockSpec((SUBLANE, d_lru), lambda b, t: (b, 0)),
            pl.BlockSpec((1, 1, d_lru), lambda b, t: (b, 0, 0)),
        ],
        out_shape=[
            jax.ShapeDtypeStruct((rows, d), F32),
            state8, state8,
            jax.ShapeDtypeStruct((nb, 1, d_lru), F32),
        ],
        scratch_shapes=[
            pltpu.VMEM((SUBLANE, d_lru), F32),
            pltpu.VMEM((SUBLANE, d_lru), F32),
            pltpu.VMEM((1, d_lru), F32),
            pltpu.VMEM((tt, d), BF16),
            pltpu.VMEM((tt, N_Z * MXU_DIM), F32),
            pltpu.VMEM((tt, N_Z * MXU_DIM), F32),
            pltpu.VMEM((tt, 2 * MXU_DIM), BF16),
            pltpu.VMEM((tt, 2 * MXU_DIM), BF16),
        ],
        compiler_params=pltpu.CompilerParams(
            dimension_semantics=("parallel", "arbitrary"), vmem_limit_bytes=VMEM_LIMIT),
        name="mix",
    )(h, p["pre_g"], p["win"], p["cw"], p["cb"], p["wax"], p["ba"], p["bx"], p["lam"], p["sw"], p["lg"],
      p["sg"], p["gmat"], p["wout"], p["post_g"], xh0, cvh0, h0)


def _block_diag_gates(w_a, w_x):
    nh, blk, _ = w_a.shape
    per = MXU_DIM // blk
    eye = jnp.eye(per, dtype=w_a.dtype)

    def bd(w):
        w = w.reshape(nh // per, per, blk, blk)
        return jnp.einsum("cpij,pq->cpiqj", w, eye).reshape(nh // per, MXU_DIM, MXU_DIM)

    return jnp.concatenate([bd(w_a), bd(w_x)], axis=-1).astype(BF16)


def kernel(x, meta_tokens, ffn1_pre_g, ffn1_w_gate, ffn1_w_up, ffn1_w_down, ffn1_post_g, mix_pre_g, w_in,
           lru_conv_w, lru_conv_b, lru_w_a, lru_b_a, lru_w_x, lru_b_x, lru_lambda, sconv_w, lru_out_g,
           sconv_out_g, w_out, mix_post_g, ffn2_pre_g, ffn2_w_gate, ffn2_w_up, ffn2_w_down, ffn2_post_g):
    bn, seq, d = x.shape
    n_meta = meta_tokens.shape[0]
    d_lru = lru_conv_w.shape[-1]
    d_sc = sconv_w.shape[-1]
    assert ffn1_pre_g.shape[0] == 1, "single layer"
    assert n_meta % SUBLANE == 0 and n_meta >= SUBLANE
    assert d_lru == d_sc == N_LRU_HEADS * HEAD_DIM == N_SCONV_GROUPS * HEAD_DIM
    assert lru_w_a.shape[-1] == HEAD_DIM and d_lru % MXU_DIM == 0
    assert w_in.shape[-1] == N_Z * d_lru and w_out.shape[1] == 2 * d_lru

    w1 = tuple(w[0].astype(BF16) for w in (ffn1_w_gate, ffn1_w_up, ffn1_w_down))
    w2 = tuple(w[0].astype(BF16) for w in (ffn2_w_gate, ffn2_w_up, ffn2_w_down))
    nc = d_lru // MXU_DIM
    head = jnp.arange(MXU_DIM) // HEAD_DIM
    mp = {
        "pre_g": mix_pre_g,
        "win": w_in[0].reshape(d, N_Z, nc, MXU_DIM).transpose(0, 2, 1, 3).reshape(d, N_Z * d_lru).astype(BF16),
        "cw": lru_conv_w[0], "cb": lru_conv_b,
        "wax": _block_diag_gates(lru_w_a[0], lru_w_x[0]), "ba": lru_b_a, "bx": lru_b_x, "lam": lru_lambda,
        "sw": sconv_w[0], "lg": lru_out_g, "sg": sconv_out_g,
        "gmat": (head[:, None] == head[None, :]).astype(BF16),
        "wout": w_out[0].reshape(2, nc, MXU_DIM, d).transpose(1, 0, 2, 3).reshape(2 * d_lru, d).astype(BF16),
        "post_g": mix_post_g,
    }
    ffn_tiles = dict(tf=512, rc_edge=512)

    hm = _ffn(meta_tokens, ffn1_pre_g, *w1, ffn1_post_g, tm=n_meta, rc_edge=n_meta, tf=ffn_tiles["tf"])
    zero8 = jnp.zeros((SUBLANE, d_lru), F32)
    _, xt, cvt, hl = _mix(hm, mp, zero8, zero8, jnp.zeros((1, d_lru), F32), nb=1, tt=n_meta)

    h = x.reshape(bn * seq, d)
    h = _ffn(h, ffn1_pre_g, *w1, ffn1_post_g, tm=1024, **ffn_tiles)
    h, _, _, _ = _mix(h, mp, xt, cvt, hl[0], nb=bn, tt=512)
    h = _ffn(h, ffn2_pre_g, *w2, ffn2_post_g, tm=1024, **ffn_tiles)
    return h.reshape(bn, seq, d)
```

```python
import functools

import jax
import jax.numpy as jnp
from jax import lax
from jax.experimental import pallas as pl
from jax.experimental.pallas import tpu as pltpu

EPS = 1e-6
N_LRU_HEADS = 16
N_SCONV_GROUPS = 16
HEAD_DIM = 64
LRU_C = 8.0
SUBLANE = 8
MXU_DIM = 256
VMEM_LIMIT = 56 * 1024 * 1024
N_Z = 5

F32 = jnp.float32
BF16 = jnp.bfloat16


def _sigmoid(x):
    return 0.5 + 0.5 * jnp.tanh(0.5 * x)


def _rms(x, g):
    ms = jnp.mean(x * x, axis=-1, keepdims=True)
    return x * lax.rsqrt(ms + EPS) * g


def _ffn_step(h_ref, pre_g_ref, wg_ref, wu_ref, wd_ref, post_g_ref, o_ref, xn_ref, *, first, last, width, chunks):
    r0 = 0
    for rc in chunks:
        rows = pl.ds(r0, rc)
        r0 += rc
        if first:
            xn = _rms(h_ref[rows, :], pre_g_ref[...]).astype(BF16)
            xn_ref[rows, :] = xn
        else:
            xn = xn_ref[rows, :]
        g = jnp.dot(xn, wg_ref[:, :width], preferred_element_type=F32)
        u = jnp.dot(xn, wu_ref[:, :width], preferred_element_type=F32)
        a = (g * _sigmoid(g) * u).astype(BF16)
        acc = jnp.dot(a, wd_ref[:width, :], preferred_element_type=F32)
        if not first:
            acc = o_ref[rows, :] + acc
        if last:
            acc = h_ref[rows, :] + _rms(acc, 0.5 * post_g_ref[...])
        o_ref[rows, :] = acc


def _ffn_kernel(h_ref, pre_g_ref, wg_ref, wu_ref, wd_ref, post_g_ref, o_ref, xn_ref, *, rem, edge_chunks):
    j = pl.program_id(1)
    nj = pl.num_programs(1)
    refs = (h_ref, pre_g_ref, wg_ref, wu_ref, wd_ref, post_g_ref, o_ref, xn_ref)
    tm = h_ref.shape[0]
    tf = wg_ref.shape[1]

    @pl.when(j == 0)
    def _():
        _ffn_step(*refs, first=True, last=False, width=tf, chunks=edge_chunks)

    @pl.when(jnp.logical_and(j > 0, j < nj - 1))
    def _():
        _ffn_step(*refs, first=False, last=False, width=tf, chunks=(tm,))

    @pl.when(j == nj - 1)
    def _():
        _ffn_step(*refs, first=False, last=True, width=rem, chunks=edge_chunks)


def _ffn(h, pre_g, wg, wu, wd, post_g, *, tm, tf, edge_chunks):
    m, d = h.shape
    f = wg.shape[1]
    nj = pl.cdiv(f, tf)
    rem = f - (nj - 1) * tf
    assert nj >= 2 and rem % 128 == 0 and m % tm == 0 and sum(edge_chunks) == tm
    return pl.pallas_call(
        functools.partial(_ffn_kernel, rem=rem, edge_chunks=edge_chunks),
        grid=(m // tm, nj),
        in_specs=[
            pl.BlockSpec((tm, d), lambda i, j: (i, 0)),
            pl.BlockSpec((1, d), lambda i, j: (0, 0)),
            pl.BlockSpec((d, tf), lambda i, j: (0, j)),
            pl.BlockSpec((d, tf), lambda i, j: (0, j)),
            pl.BlockSpec((tf, d), lambda i, j: (j, 0)),
            pl.BlockSpec((1, d), lambda i, j: (0, 0)),
        ],
        out_specs=pl.BlockSpec((tm, d), lambda i, j: (i, 0)),
        out_shape=jax.ShapeDtypeStruct((m, d), F32),
        scratch_shapes=[pltpu.VMEM((tm, d), BF16)],
        compiler_params=pltpu.CompilerParams(
            dimension_semantics=("parallel", "arbitrary"), vmem_limit_bytes=VMEM_LIMIT),
        name="ffn",
    )(h, pre_g, wg, wu, wd, post_g)


def _shift_rows(x, k, head):
    r = pltpu.roll(x, k, 0)
    hr = pltpu.roll(head, k, 0)
    row = lax.broadcasted_iota(jnp.int32, head.shape, 0)
    first = jnp.where(row < k, hr, r[:SUBLANE])
    if x.shape[0] == SUBLANE:
        return first
    return jnp.concatenate([first, r[SUBLANE:]], axis=0)


def _shift_fill(x, s, fill):
    tt = x.shape[0]
    r = pltpu.roll(x, s, 0)
    row = lax.broadcasted_iota(jnp.int32, (SUBLANE,) + x.shape[1:], 0)
    first = jnp.where(row < s, fill, r[:SUBLANE])
    if tt == SUBLANE:
        return first
    return jnp.concatenate([first, r[SUBLANE:]], axis=0)


def _window_steps(a, u, steps):
    for s in steps:
        u = u + a * _shift_fill(u, s, 0.0)
        a = a * _shift_fill(a, s, 1.0)
    return a, u


def _scan_groups(a, u, h_in):
    hs = []
    h = h_in
    for g in range(a.shape[0] // SUBLANE):
        rows = slice(g * SUBLANE, (g + 1) * SUBLANE)
        h = a[rows] * h + u[rows]
        hs.append(h)
    return jnp.concatenate(hs, axis=0)


def _group_mean_sq(y, gmat):
    y2 = y * y
    hi = y2.astype(BF16)
    lo = (y2 - hi.astype(F32)).astype(BF16)
    s = jnp.dot(hi, gmat, preferred_element_type=F32) + jnp.dot(lo, gmat, preferred_element_type=F32)
    return s * (1.0 / HEAD_DIM)


def _gelu_tanh(x):
    return 0.5 * x * (1.0 + jnp.tanh(0.7978845608028654 * (x + 0.044715 * (x * x * x))))


def _mix_kernel(h_ref, pre_g_ref, win_ref, cw_ref, cb_ref, wax_ref, ba_ref, bx_ref, lam_ref, sw_ref, lg_ref,
                sg_ref, gmat_ref, wout_ref, post_g_ref, xh0_ref, cvh0_ref, h0_ref,
                o_ref, xt_ref, cvt_ref, hl_ref,
                xh_s, cvh_s, hc_s, un_s, z0_s, z1_s, m0_s, m1_s, *, d_lru):
    @pl.when(pl.program_id(1) == 0)
    def _():
        xh_s[...] = xh0_ref[...]
        cvh_s[...] = cvh0_ref[...]
        hc_s[...] = h0_ref[...]

    tt, d = h_ref.shape
    w = MXU_DIM
    nc = d_lru // w
    z_slots = (z0_s, z1_s)
    m_slots = (m0_s, m1_s)
    gmat = gmat_ref[...]
    un_s[...] = _rms(h_ref[...], pre_g_ref[...]).astype(BF16)

    def project_pieces(c):
        def piece(k):
            def run():
                col = k * d_lru + c * w
                z_slots[c % 2][:, k * w:(k + 1) * w] = jnp.dot(
                    un_s[...], win_ref[:, col:col + w], preferred_element_type=F32)
            return run
        return [piece(k) for k in range(N_Z)]

    def outproj_pieces(c):
        half = d // 2

        def piece(n):
            def run():
                ncols = slice(n * half, (n + 1) * half)
                m = m_slots[c % 2]
                part = (jnp.dot(m[:, :w], wout_ref[c * w:(c + 1) * w, ncols], preferred_element_type=F32)
                        + jnp.dot(m[:, w:], wout_ref[d_lru + c * w:d_lru + (c + 1) * w, ncols],
                                  preferred_element_type=F32))
                if c == 0:
                    o_ref[:, ncols] = part
                else:
                    o_ref[:, ncols] += part
            return run
        return [piece(0), piece(1)]

    def mixer(c):
        z = z_slots[c % 2]
        m = m_slots[c % 2]
        cols = slice(c * w, (c + 1) * w)
        x = z[:, w:2 * w]
        head = xh_s[:, cols]
        xc = (cb_ref[:, cols] + cw_ref[3:4, cols] * x
              + cw_ref[2:3, cols] * _shift_rows(x, 1, head)
              + cw_ref[1:2, cols] * _shift_rows(x, 2, head)
              + cw_ref[0:1, cols] * _shift_rows(x, 3, head))
        xh_s[:, cols] = x[tt - SUBLANE:]
        pre = jnp.dot(xc.astype(BF16), wax_ref[c], preferred_element_type=F32)
        yield
        cv = z[:, 3 * w:4 * w] * z[:, 4 * w:5 * w]
        chead = cvh_s[:, cols]
        sc = z[:, 2 * w:3 * w] * (sw_ref[2:3, cols] * cv
                                  + sw_ref[1:2, cols] * _shift_rows(cv, 1, chead)
                                  + sw_ref[0:1, cols] * _shift_rows(cv, 2, chead))
        cvh_s[:, cols] = cv[tt - SUBLANE:]
        sc_ms = _group_mean_sq(sc, gmat)
        yield
        ga = _sigmoid(pre[:, :w] + ba_ref[:, cols])
        gx = _sigmoid(pre[:, w:] + bx_ref[:, cols])
        nl = -lam_ref[:, cols]
        sp = jnp.maximum(nl, 0.0) + jnp.log1p(jnp.exp(-jnp.abs(nl)))
        log_a = (-LRU_C * ga) * sp
        a = jnp.exp(log_a)
        mult = jnp.sqrt(-jnp.tanh(log_a) * (a * a + 1.0))
        u = mult * (gx * xc)
        yield
        a, u = _window_steps(a, u, (1, 2))
        yield
        a, u = _window_steps(a, u, (4,))
        h = _scan_groups(a, u, hc_s[:, cols])
        hc_s[:, cols] = h[tt - 1:]
        yield
        lo = h * _gelu_tanh(z[:, 0:w])
        lo_ms = _group_mean_sq(lo, gmat)
        yield
        m[:, w:] = (sc * lax.rsqrt(sc_ms + EPS) * sg_ref[:, cols]).astype(BF16)
        m[:, :w] = (lo * lax.rsqrt(lo_ms + EPS) * lg_ref[:, cols]).astype(BF16)
        yield

    for run in project_pieces(0):
        run()
    pending = []
    for c in range(nc):
        if c + 1 < nc:
            pending += project_pieces(c + 1)
        for _ in mixer(c):
            if pending:
                pending.pop(0)()
        while pending:
            pending.pop(0)()
        pending = outproj_pieces(c)
    for run in pending:
        run()

    o_ref[...] = h_ref[...] + _rms(o_ref[...], post_g_ref[...])
    xt_ref[...] = xh_s[...]
    cvt_ref[...] = cvh_s[...]
    hl_ref[0] = hc_s[...]


def _mix(h, p, xh0, cvh0, h0, *, nb, tt):
    rows, d = h.shape
    d_lru = p["cb"].shape[1]
    nt = rows // (nb * tt)
    assert rows == nb * nt * tt
    const = lambda b, t: (0, 0)
    once = pl.Buffered(1)
    vec = pl.BlockSpec((1, d_lru), const)
    gvec = pl.BlockSpec((1, d), const)
    state8 = jax.ShapeDtypeStruct((nb * SUBLANE, d_lru), F32)
    return pl.pallas_call(
        functools.partial(_mix_kernel, d_lru=d_lru),
        grid=(nb, nt),
        in_specs=[
            pl.BlockSpec((tt, d), lambda b, t: (b * nt + t, 0)),
            gvec,
            pl.BlockSpec(p["win"].shape, const, pipeline_mode=once),
            pl.BlockSpec(p["cw"].shape, const), vec,
            pl.BlockSpec(p["wax"].shape, lambda b, t: (0, 0, 0)),
            vec, vec, vec,
            pl.BlockSpec(p["sw"].shape, const), vec, vec,
            pl.BlockSpec((MXU_DIM, MXU_DIM), const),
            pl.BlockSpec(p["wout"].shape, const, pipeline_mode=once),
            gvec,
            pl.BlockSpec((SUBLANE, d_lru), const),
            pl.BlockSpec((SUBLANE, d_lru), const),
            vec,
        ],
        out_specs=[
            pl.BlockSpec((tt, d), lambda b, t: (b * nt + t, 0)),
            pl.BlockSpec((SUBLANE, d_lru), lambda b, t: (b, 0)),
            pl.BlockSpec((SUBLANE, d_lru), lambda b, t: (b, 0)),
            pl.BlockSpec((1, 1, d_lru), lambda b, t: (b, 0, 0)),
        ],
        out_shape=[
            jax.ShapeDtypeStruct((rows, d), F32),
            state8, state8,
            jax.ShapeDtypeStruct((nb, 1, d_lru), F32),
        ],
        scratch_shapes=[
            pltpu.VMEM((SUBLANE, d_lru), F32),
            pltpu.VMEM((SUBLANE, d_lru), F32),
            pltpu.VMEM((1, d_lru), F32),
            pltpu.VMEM((tt, d), BF16),
            pltpu.VMEM((tt, N_Z * MXU_DIM), F32),
            pltpu.VMEM((tt, N_Z * MXU_DIM), F32),
            pltpu.VMEM((tt, 2 * MXU_DIM), BF16),
            pltpu.VMEM((tt, 2 * MXU_DIM), BF16),
        ],
        compiler_params=pltpu.CompilerParams(
            dimension_semantics=("parallel", "arbitrary"), vmem_limit_bytes=VMEM_LIMIT),
        name="mix",
    )(h, p["pre_g"], p["win"], p["cw"], p["cb"], p["wax"], p["ba"], p["bx"], p["lam"], p["sw"], p["lg"],
      p["sg"], p["gmat"], p["wout"], p["post_g"], xh0, cvh0, h0)


def _block_diag_gates(w_a, w_x):
    nh, blk, _ = w_a.shape
    per = MXU_DIM // blk
    eye = jnp.eye(per, dtype=w_a.dtype)

    def bd(w):
        w = w.reshape(nh // per, per, blk, blk)
        return jnp.einsum("cpij,pq->cpiqj", w, eye).reshape(nh // per, MXU_DIM, MXU_DIM)

    return jnp.concatenate([bd(w_a), bd(w_x)], axis=-1).astype(BF16)


def kernel(x, meta_tokens, ffn1_pre_g, ffn1_w_gate, ffn1_w_up, ffn1_w_down, ffn1_post_g, mix_pre_g, w_in,
           lru_conv_w, lru_conv_b, lru_w_a, lru_b_a, lru_w_x, lru_b_x, lru_lambda, sconv_w, lru_out_g,
           sconv_out_g, w_out, mix_post_g, ffn2_pre_g, ffn2_w_gate, ffn2_w_up, ffn2_w_down, ffn2_post_g):
    bn, seq, d = x.shape
    n_meta = meta_tokens.shape[0]
    d_lru = lru_conv_w.shape[-1]
    d_sc = sconv_w.shape[-1]
    assert ffn1_pre_g.shape[0] == 1, "single layer"
    assert n_meta % SUBLANE == 0 and n_meta >= SUBLANE
    assert d_lru == d_sc == N_LRU_HEADS * HEAD_DIM == N_SCONV_GROUPS * HEAD_DIM
    assert lru_w_a.shape[-1] == HEAD_DIM and d_lru % MXU_DIM == 0
    assert w_in.shape[-1] == N_Z * d_lru and w_out.shape[1] == 2 * d_lru

    w1 = tuple(w[0].astype(BF16) for w in (ffn1_w_gate, ffn1_w_up, ffn1_w_down))
    w2 = tuple(w[0].astype(BF16) for w in (ffn2_w_gate, ffn2_w_up, ffn2_w_down))
    head = jnp.arange(MXU_DIM) // HEAD_DIM
    mp = {
        "pre_g": mix_pre_g,
        "win": w_in[0].astype(BF16),
        "cw": lru_conv_w[0], "cb": lru_conv_b,
        "wax": _block_diag_gates(lru_w_a[0], lru_w_x[0]), "ba": lru_b_a, "bx": lru_b_x, "lam": lru_lambda,
        "sw": sconv_w[0], "lg": lru_out_g, "sg": sconv_out_g,
        "gmat": (head[:, None] == head[None, :]).astype(BF16),
        "wout": w_out[0].astype(BF16),
        "post_g": mix_post_g,
    }
    ffn_tiles = dict(tm=1024, tf=512, edge_chunks=(512, 512))

    hm = _ffn(meta_tokens, ffn1_pre_g, *w1, ffn1_post_g, tm=n_meta, tf=ffn_tiles["tf"], edge_chunks=(n_meta,))
    zero8 = jnp.zeros((SUBLANE, d_lru), F32)
    _, xt, cvt, hl = _mix(hm, mp, zero8, zero8, jnp.zeros((1, d_lru), F32), nb=1, tt=n_meta)

    h = x.reshape(bn * seq, d)
    h = _ffn(h, ffn1_pre_g, *w1, ffn1_post_g, **ffn_tiles)
    h, _, _, _ = _mix(h, mp, xt, cvt, hl[0], nb=bn, tt=512)
    h = _ffn(h, ffn2_pre_g, *w2, ffn2_post_g, **ffn_tiles)
    return h.reshape(bn, seq, d)
```

```python
import functools

import jax
import jax.numpy as jnp
from jax import lax
from jax.experimental import pallas as pl
from jax.experimental.pallas import tpu as pltpu

EPS = 1e-6
N_LRU_HEADS = 16
N_SCONV_GROUPS = 16
HEAD_DIM = 64
LRU_C = 8.0
SUBLANE = 8
MXU_DIM = 256
VMEM_LIMIT = 56 * 1024 * 1024
N_Z = 5

F32 = jnp.float32
BF16 = jnp.bfloat16


def _sigmoid(x):
    return 0.5 + 0.5 * jnp.tanh(0.5 * x)


def _rms(x, g):
    ms = jnp.mean(x * x, axis=-1, keepdims=True)
    return x * lax.rsqrt(ms + EPS) * g


def _ffn_step(h_ref, pre_g_ref, wg_ref, wu_ref, wd_ref, post_g_ref, o_ref, xn_ref, *, first, last, width, chunks):
    r0 = 0
    for rc in chunks:
        rows = pl.ds(r0, rc)
        r0 += rc
        if first:
            xn = _rms(h_ref[rows, :], pre_g_ref[...]).astype(BF16)
            xn_ref[rows, :] = xn
        else:
            xn = xn_ref[rows, :]
        g = jnp.dot(xn, wg_ref[:, :width], preferred_element_type=F32)
        u = jnp.dot(xn, wu_ref[:, :width], preferred_element_type=F32)
        a = (g * _sigmoid(g) * u).astype(BF16)
        acc = jnp.dot(a, wd_ref[:width, :], preferred_element_type=F32)
        if not first:
            acc = o_ref[rows, :] + acc
        o_ref[rows, :] = acc
        if last:
            o_ref[rows, :] = h_ref[rows, :] + _rms(o_ref[rows, :], 0.5 * post_g_ref[...])


def _ffn_kernel(h_ref, pre_g_ref, wg_ref, wu_ref, wd_ref, post_g_ref, o_ref, xn_ref, *, rem, edge_chunks):
    j = pl.program_id(1)
    nj = pl.num_programs(1)
    refs = (h_ref, pre_g_ref, wg_ref, wu_ref, wd_ref, post_g_ref, o_ref, xn_ref)
    tm = h_ref.shape[0]
    tf = wg_ref.shape[1]

    @pl.when(j == 0)
    def _():
        _ffn_step(*refs, first=True, last=False, width=tf, chunks=edge_chunks)

    @pl.when(jnp.logical_and(j > 0, j < nj - 1))
    def _():
        _ffn_step(*refs, first=False, last=False, width=tf, chunks=(tm,))

    @pl.when(j == nj - 1)
    def _():
        _ffn_step(*refs, first=False, last=True, width=rem, chunks=edge_chunks)


def _ffn(h, pre_g, wg, wu, wd, post_g, *, tm, tf, edge_chunks):
    m, d = h.shape
    f = wg.shape[1]
    nj = pl.cdiv(f, tf)
    rem = f - (nj - 1) * tf
    assert nj >= 2 and rem % 128 == 0 and m % tm == 0 and sum(edge_chunks) == tm
    return pl.pallas_call(
        functools.partial(_ffn_kernel, rem=rem, edge_chunks=edge_chunks),
        grid=(m // tm, nj),
        in_specs=[
            pl.BlockSpec((tm, d), lambda i, j: (i, 0)),
            pl.BlockSpec((1, d), lambda i, j: (0, 0)),
            pl.BlockSpec((d, tf), lambda i, j: (0, j)),
            pl.BlockSpec((d, tf), lambda i, j: (0, j)),
            pl.BlockSpec((tf, d), lambda i, j: (j, 0)),
            pl.BlockSpec((1, d), lambda i, j: (0, 0)),
        ],
        out_specs=pl.BlockSpec((tm, d), lambda i, j: (i, 0)),
        out_shape=jax.ShapeDtypeStruct((m, d), F32),
        scratch_shapes=[pltpu.VMEM((tm, d), BF16)],
        compiler_params=pltpu.CompilerParams(
            dimension_semantics=("parallel", "arbitrary"), vmem_limit_bytes=VMEM_LIMIT),
        name="ffn",
    )(h, pre_g, wg, wu, wd, post_g)


def _shift_rows(x, k, head):
    r = pltpu.roll(x, k, 0)
    hr = pltpu.roll(head, k, 0)
    row = lax.broadcasted_iota(jnp.int32, head.shape, 0)
    first = jnp.where(row < k, hr, r[:SUBLANE])
    if x.shape[0] == SUBLANE:
        return first
    return jnp.concatenate([first, r[SUBLANE:]], axis=0)


def _shift_fill(x, s, fill):
    tt = x.shape[0]
    r = pltpu.roll(x, s, 0)
    row = lax.broadcasted_iota(jnp.int32, (SUBLANE,) + x.shape[1:], 0)
    first = jnp.where(row < s, fill, r[:SUBLANE])
    if tt == SUBLANE:
        return first
    return jnp.concatenate([first, r[SUBLANE:]], axis=0)


def _window_steps(a, u, steps):
    for s in steps:
        u = u + a * _shift_fill(u, s, 0.0)
        a = a * _shift_fill(a, s, 1.0)
    return a, u


def _scan_groups(a, u, h_in):
    hs = []
    h = h_in
    for g in range(a.shape[0] // SUBLANE):
        rows = slice(g * SUBLANE, (g + 1) * SUBLANE)
        h = a[rows] * h + u[rows]
        hs.append(h)
    return jnp.concatenate(hs, axis=0)


def _group_mean_sq(y, gmat):
    y2 = y * y
    hi = y2.astype(BF16)
    lo = (y2 - hi.astype(F32)).astype(BF16)
    s = jnp.dot(jnp.concatenate([hi, lo], axis=1), gmat, preferred_element_type=F32)
    return s * (1.0 / HEAD_DIM)


def _gelu_tanh(x):
    return 0.5 * x * (1.0 + jnp.tanh(0.7978845608028654 * (x + 0.044715 * (x * x * x))))


def _mix_kernel(h_ref, pre_g_ref, win_ref, cw_ref, cb_ref, wax_ref, ba_ref, bx_ref, lam_ref, sw_ref, lg_ref,
                sg_ref, gmat_ref, wout_ref, post_g_ref, xh0_ref, cvh0_ref, h0_ref,
                o_ref, xt_ref, cvt_ref, hl_ref,
                xh_s, cvh_s, hc_s, un_s, z0_s, z1_s, m0_s, m1_s, *, d_lru):
    @pl.when(pl.program_id(1) == 0)
    def _():
        xh_s[...] = xh0_ref[...]
        cvh_s[...] = cvh0_ref[...]
        hc_s[...] = h0_ref[...]

    tt, d = h_ref.shape
    w = MXU_DIM
    nc = d_lru // w
    z_slots = (z0_s, z1_s)
    m_slots = (m0_s, m1_s)
    gmat = gmat_ref[...]
    un_s[...] = _rms(h_ref[...], pre_g_ref[...]).astype(BF16)

    def project_pieces(c):
        def piece(k):
            def run():
                col = k * d_lru + c * w
                z_slots[c % 2][:, k * w:(k + 1) * w] = jnp.dot(
                    un_s[...], win_ref[:, col:col + w], preferred_element_type=F32)
            return run
        return [piece(k) for k in range(N_Z)]

    def outproj_pieces(c):
        half = d // 2

        def piece(n):
            def run():
                ncols = slice(n * half, (n + 1) * half)
                part = jnp.dot(m_slots[c % 2][...], wout_ref[c * 2 * w:(c + 1) * 2 * w, ncols],
                               preferred_element_type=F32)
                if c == 0:
                    o_ref[:, ncols] = part
                else:
                    o_ref[:, ncols] += part
            return run
        return [piece(0), piece(1)]

    def mixer(c):
        z = z_slots[c % 2]
        m = m_slots[c % 2]
        cols = slice(c * w, (c + 1) * w)
        x = z[:, w:2 * w]
        head = xh_s[:, cols]
        xc = (cb_ref[:, cols] + cw_ref[3:4, cols] * x
              + cw_ref[2:3, cols] * _shift_rows(x, 1, head)
              + cw_ref[1:2, cols] * _shift_rows(x, 2, head)
              + cw_ref[0:1, cols] * _shift_rows(x, 3, head))
        xh_s[:, cols] = x[tt - SUBLANE:]
        pre = jnp.dot(xc.astype(BF16), wax_ref[c], preferred_element_type=F32)
        yield
        cv = z[:, 3 * w:4 * w] * z[:, 4 * w:5 * w]
        chead = cvh_s[:, cols]
        sc = z[:, 2 * w:3 * w] * (sw_ref[2:3, cols] * cv
                                  + sw_ref[1:2, cols] * _shift_rows(cv, 1, chead)
                                  + sw_ref[0:1, cols] * _shift_rows(cv, 2, chead))
        cvh_s[:, cols] = cv[tt - SUBLANE:]
        sc_ms = _group_mean_sq(sc, gmat)
        yield
        ga = _sigmoid(pre[:, :w] + ba_ref[:, cols])
        gx = _sigmoid(pre[:, w:] + bx_ref[:, cols])
        nl = -lam_ref[:, cols]
        sp = jnp.maximum(nl, 0.0) + jnp.log1p(jnp.exp(-jnp.abs(nl)))
        log_a = (-LRU_C * ga) * sp
        a = jnp.exp(log_a)
        mult = jnp.sqrt(-jnp.tanh(log_a) * (a * a + 1.0))
        u = mult * (gx * xc)
        yield
        a, u = _window_steps(a, u, (1, 2))
        yield
        a, u = _window_steps(a, u, (4,))
        h = _scan_groups(a, u, hc_s[:, cols])
        hc_s[:, cols] = h[tt - 1:]
        yield
        lo = h * _gelu_tanh(z[:, 0:w])
        lo_ms = _group_mean_sq(lo, gmat)
        yield
        m[:, w:] = (sc * lax.rsqrt(sc_ms + EPS) * sg_ref[:, cols]).astype(BF16)
        m[:, :w] = (lo * lax.rsqrt(lo_ms + EPS) * lg_ref[:, cols]).astype(BF16)
        yield

    for run in project_pieces(0):
        run()
    pending = []
    for c in range(nc):
        if c + 1 < nc:
            pending += project_pieces(c + 1)
        for _ in mixer(c):
            if pending:
                pending.pop(0)()
        while pending:
            pending.pop(0)()
        pending = outproj_pieces(c)
    for run in pending:
        run()

    o_ref[...] = h_ref[...] + _rms(o_ref[...], post_g_ref[...])
    xt_ref[...] = xh_s[...]
    cvt_ref[...] = cvh_s[...]
    hl_ref[0] = hc_s[...]


def _mix(h, p, xh0, cvh0, h0, *, nb, tt):
    rows, d = h.shape
    d_lru = p["cb"].shape[1]
    nt = rows // (nb * tt)
    assert rows == nb * nt * tt
    const = lambda b, t: (0, 0)
    once = pl.Buffered(1)
    vec = pl.BlockSpec((1, d_lru), const)
    gvec = pl.BlockSpec((1, d), const)
    state8 = jax.ShapeDtypeStruct((nb * SUBLANE, d_lru), F32)
    return pl.pallas_call(
        functools.partial(_mix_kernel, d_lru=d_lru),
        grid=(nb, nt),
        in_specs=[
            pl.BlockSpec((tt, d), lambda b, t: (b * nt + t, 0)),
            gvec,
            pl.BlockSpec(p["win"].shape, const, pipeline_mode=once),
            pl.BlockSpec(p["cw"].shape, const), vec,
            pl.BlockSpec(p["wax"].shape, lambda b, t: (0, 0, 0)),
            vec, vec, vec,
            pl.BlockSpec(p["sw"].shape, const), vec, vec,
            pl.BlockSpec((2 * MXU_DIM, MXU_DIM), const),
            pl.BlockSpec(p["wout"].shape, const, pipeline_mode=once),
            gvec,
            pl.BlockSpec((SUBLANE, d_lru), const),
            pl.BlockSpec((SUBLANE, d_lru), const),
            vec,
        ],
        out_specs=[
            pl.BlockSpec((tt, d), lambda b, t: (b * nt + t, 0)),
            pl.BlockSpec((SUBLANE, d_lru), lambda b, t: (b, 0)),
            pl.BlockSpec((SUBLANE, d_lru), lambda b, t: (b, 0)),
            pl.BlockSpec((1, 1, d_lru), lambda b, t: (b, 0, 0)),
        ],
        out_shape=[
            jax.ShapeDtypeStruct((rows, d), F32),
            state8, state8,
            jax.ShapeDtypeStruct((nb, 1, d_lru), F32),
        ],
        scratch_shapes=[
            pltpu.VMEM((SUBLANE, d_lru), F32),
            pltpu.VMEM((SUBLANE, d_lru), F32),
            pltpu.VMEM((1, d_lru), F32),
            pltpu.VMEM((tt, d), BF16),
            pltpu.VMEM((tt, N_Z * MXU_DIM), F32),
            pltpu.VMEM((tt, N_Z * MXU_DIM), F32),
            pltpu.VMEM((tt, 2 * MXU_DIM), BF16),
            pltpu.VMEM((tt, 2 * MXU_DIM), BF16),
        ],
        compiler_params=pltpu.CompilerParams(
            dimension_semantics=("parallel", "arbitrary"), vmem_limit_bytes=VMEM_LIMIT),
        name="mix",
    )(h, p["pre_g"], p["win"], p["cw"], p["cb"], p["wax"], p["ba"], p["bx"], p["lam"], p["sw"], p["lg"],
      p["sg"], p["gmat"], p["wout"], p["post_g"], xh0, cvh0, h0)


def _block_diag_gates(w_a, w_x):
    nh, blk, _ = w_a.shape
    per = MXU_DIM // blk
    eye = jnp.eye(per, dtype=w_a.dtype)

    def bd(w):
        w = w.reshape(nh // per, per, blk, blk)
        return jnp.einsum("cpij,pq->cpiqj", w, eye).reshape(nh // per, MXU_DIM, MXU_DIM)

    return jnp.concatenate([bd(w_a), bd(w_x)], axis=-1).astype(BF16)


def kernel(x, meta_tokens, ffn1_pre_g, ffn1_w_gate, ffn1_w_up, ffn1_w_down, ffn1_post_g, mix_pre_g, w_in,
           lru_conv_w, lru_conv_b, lru_w_a, lru_b_a, lru_w_x, lru_b_x, lru_lambda, sconv_w, lru_out_g,
           sconv_out_g, w_out, mix_post_g, ffn2_pre_g, ffn2_w_gate, ffn2_w_up, ffn2_w_down, ffn2_post_g):
    bn, seq, d = x.shape
    n_meta = meta_tokens.shape[0]
    d_lru = lru_conv_w.shape[-1]
    d_sc = sconv_w.shape[-1]
    assert ffn1_pre_g.shape[0] == 1, "single layer"
    assert n_meta % SUBLANE == 0 and n_meta >= SUBLANE
    assert d_lru == d_sc == N_LRU_HEADS * HEAD_DIM == N_SCONV_GROUPS * HEAD_DIM
    assert lru_w_a.shape[-1] == HEAD_DIM and d_lru % MXU_DIM == 0
    assert w_in.shape[-1] == N_Z * d_lru and w_out.shape[1] == 2 * d_lru

    w1 = tuple(w[0].astype(BF16) for w in (ffn1_w_gate, ffn1_w_up, ffn1_w_down))
    w2 = tuple(w[0].astype(BF16) for w in (ffn2_w_gate, ffn2_w_up, ffn2_w_down))
    head = jnp.arange(MXU_DIM) // HEAD_DIM
    mp = {
        "pre_g": mix_pre_g,
        "win": w_in[0].astype(BF16),
        "cw": lru_conv_w[0], "cb": lru_conv_b,
        "wax": _block_diag_gates(lru_w_a[0], lru_w_x[0]), "ba": lru_b_a, "bx": lru_b_x, "lam": lru_lambda,
        "sw": sconv_w[0], "lg": lru_out_g, "sg": sconv_out_g,
        "gmat": jnp.tile((head[:, None] == head[None, :]).astype(BF16), (2, 1)),
        "wout": w_out[0].reshape(2, d_lru // MXU_DIM, MXU_DIM, d).transpose(1, 0, 2, 3)
                        .reshape(2 * d_lru, d).astype(BF16),
        "post_g": mix_post_g,
    }
    ffn_tiles = dict(tm=1024, tf=512, edge_chunks=(512, 512))

    hm = _ffn(meta_tokens, ffn1_pre_g, *w1, ffn1_post_g, tm=n_meta, tf=ffn_tiles["tf"], edge_chunks=(n_meta,))
    zero8 = jnp.zeros((SUBLANE, d_lru), F32)
    _, xt, cvt, hl = _mix(hm, mp, zero8, zero8, jnp.zeros((1, d_lru), F32), nb=1, tt=n_meta)

    h = x.reshape(bn * seq, d)
    h = _ffn(h, ffn1_pre_g, *w1, ffn1_post_g, **ffn_tiles)
    h, _, _, _ = _mix(h, mp, xt, cvt, hl[0], nb=bn, tt=512)
    h = _ffn(h, ffn2_pre_g, *w2, ffn2_post_g, **ffn_tiles)
    return h.reshape(bn, seq, d)
```

```python
import functools

import jax
import jax.numpy as jnp
from jax import lax
from jax.experimental import pallas as pl
from jax.experimental.pallas import tpu as pltpu

EPS = 1e-6
N_LRU_HEADS = 16
N_SCONV_GROUPS = 16
HEAD_DIM = 64
LRU_C = 8.0
SUBLANE = 8
MXU_DIM = 256
VMEM_LIMIT = 56 * 1024 * 1024
N_Z = 5

F32 = jnp.float32
BF16 = jnp.bfloat16


def _sigmoid(x):
    return 0.5 + 0.5 * jnp.tanh(0.5 * x)


def _rms(x, g):
    ms = jnp.mean(x * x, axis=-1, keepdims=True)
    return x * lax.rsqrt(ms + EPS) * g


def _ffn_step(h_ref, pre_g_ref, wg_ref, wu_ref, wd_ref, post_g_ref, o_ref, xn_ref, *, first, last, width, chunks):
    r0 = 0
    for rc in chunks:
        rows = pl.ds(r0, rc)
        r0 += rc
        if first:
            xn = _rms(h_ref[rows, :], pre_g_ref[...]).astype(BF16)
            xn_ref[rows, :] = xn
        else:
            xn = xn_ref[rows, :]
        g = jnp.dot(xn, wg_ref[:, :width], preferred_element_type=F32)
        u = jnp.dot(xn, wu_ref[:, :width], preferred_element_type=F32)
        a = (g * _sigmoid(g) * u).astype(BF16)
        acc = jnp.dot(a, wd_ref[:width, :], preferred_element_type=F32)
        if not first:
            acc = o_ref[rows, :] + acc
        o_ref[rows, :] = acc
        if last:
            o_ref[rows, :] = h_ref[rows, :] + _rms(o_ref[rows, :], 0.5 * post_g_ref[...])


def _ffn_kernel(h_ref, pre_g_ref, wg_ref, wu_ref, wd_ref, post_g_ref, o_ref, xn_ref, *, rem, edge_chunks):
    j = pl.program_id(1)
    nj = pl.num_programs(1)
    refs = (h_ref, pre_g_ref, wg_ref, wu_ref, wd_ref, post_g_ref, o_ref, xn_ref)
    tm = h_ref.shape[0]
    tf = wg_ref.shape[1]

    @pl.when(j == 0)
    def _():
        _ffn_step(*refs, first=True, last=False, width=tf, chunks=edge_chunks)

    @pl.when(jnp.logical_and(j > 0, j < nj - 1))
    def _():
        _ffn_step(*refs, first=False, last=False, width=tf, chunks=(tm,))

    @pl.when(j == nj - 1)
    def _():
        _ffn_step(*refs, first=False, last=True, width=rem, chunks=edge_chunks)


def _ffn(h, pre_g, wg, wu, wd, post_g, *, tm, tf, edge_chunks):
    m, d = h.shape
    f = wg.shape[1]
    nj = pl.cdiv(f, tf)
    rem = f - (nj - 1) * tf
    assert nj >= 2 and rem % 128 == 0 and m % tm == 0 and sum(edge_chunks) == tm
    return pl.pallas_call(
        functools.partial(_ffn_kernel, rem=rem, edge_chunks=edge_chunks),
        grid=(m // tm, nj),
        in_specs=[
            pl.BlockSpec((tm, d), lambda i, j: (i, 0)),
            pl.BlockSpec((1, d), lambda i, j: (0, 0)),
            pl.BlockSpec((d, tf), lambda i, j: (0, j)),
            pl.BlockSpec((d, tf), lambda i, j: (0, j)),
            pl.BlockSpec((tf, d), lambda i, j: (j, 0)),
            pl.BlockSpec((1, d), lambda i, j: (0, 0)),
        ],
        out_specs=pl.BlockSpec((tm, d), lambda i, j: (i, 0), pipeline_mode=pl.Buffered(1)),
        out_shape=jax.ShapeDtypeStruct((m, d), F32),
        scratch_shapes=[pltpu.VMEM((tm, d), BF16)],
        compiler_params=pltpu.CompilerParams(
            dimension_semantics=("parallel", "arbitrary"), vmem_limit_bytes=VMEM_LIMIT),
        name="ffn",
    )(h, pre_g, wg, wu, wd, post_g)


def _shift_rows(x, k, head):
    r = pltpu.roll(x, k, 0)
    hr = pltpu.roll(head, k, 0)
    row = lax.broadcasted_iota(jnp.int32, head.shape, 0)
    first = jnp.where(row < k, hr, r[:SUBLANE])
    if x.shape[0] == SUBLANE:
        return first
    return jnp.concatenate([first, r[SUBLANE:]], axis=0)


def _shift_fill(x, s, fill):
    tt = x.shape[0]
    r = pltpu.roll(x, s, 0)
    row = lax.broadcasted_iota(jnp.int32, (SUBLANE,) + x.shape[1:], 0)
    first = jnp.where(row < s, fill, r[:SUBLANE])
    if tt == SUBLANE:
        return first
    return jnp.concatenate([first, r[SUBLANE:]], axis=0)


def _window_steps(a, u, steps):
    for s in steps:
        u = u + a * _shift_fill(u, s, 0.0)
        a = a * _shift_fill(a, s, 1.0)
    return a, u


def _scan_groups(a, u, h_in):
    hs = []
    h = h_in
    for g in range(a.shape[0] // SUBLANE):
        rows = slice(g * SUBLANE, (g + 1) * SUBLANE)
        h = a[rows] * h + u[rows]
        hs.append(h)
    return jnp.concatenate(hs, axis=0)


def _group_mean_sq(y, gmat):
    y2 = y * y
    hi = y2.astype(BF16)
    lo = (y2 - hi.astype(F32)).astype(BF16)
    s = jnp.dot(jnp.concatenate([hi, lo], axis=1), gmat, preferred_element_type=F32)
    return s * (1.0 / HEAD_DIM)


def _gelu_tanh(x):
    return 0.5 * x * (1.0 + jnp.tanh(0.7978845608028654 * (x + 0.044715 * (x * x * x))))


def _mix_kernel(h_ref, pre_g_ref, win_ref, cw_ref, cb_ref, wax_ref, ba_ref, bx_ref, lam_ref, sw_ref, lg_ref,
                sg_ref, gmat_ref, wout_ref, post_g_ref, xh0_ref, cvh0_ref, h0_ref,
                o_ref, xt_ref, cvt_ref, hl_ref,
                xh_s, cvh_s, hc_s, un_s, z0_s, z1_s, m0_s, m1_s, *, d_lru):
    @pl.when(pl.program_id(1) == 0)
    def _():
        xh_s[...] = xh0_ref[...]
        cvh_s[...] = cvh0_ref[...]
        hc_s[...] = h0_ref[...]

    tt, d = h_ref.shape
    w = MXU_DIM
    nc = d_lru // w
    z_slots = (z0_s, z1_s)
    m_slots = (m0_s, m1_s)
    gmat = gmat_ref[...]
    un_s[...] = _rms(h_ref[...], pre_g_ref[...]).astype(BF16)

    def project_pieces(c):
        def piece(k):
            def run():
                col = k * d_lru + c * w
                z_slots[c % 2][:, k * w:(k + 1) * w] = jnp.dot(
                    un_s[...], win_ref[:, col:col + w], preferred_element_type=F32)
            return run
        return [piece(k) for k in range(N_Z)]

    def outproj_pieces(c):
        half = d // 2

        def piece(n):
            def run():
                ncols = slice(n * half, (n + 1) * half)
                part = jnp.dot(m_slots[c % 2][...], wout_ref[c * 2 * w:(c + 1) * 2 * w, ncols],
                               preferred_element_type=F32)
                if c == 0:
                    o_ref[:, ncols] = part
                else:
                    o_ref[:, ncols] += part
            return run
        return [piece(0), piece(1)]

    def mixer(c):
        z = z_slots[c % 2]
        m = m_slots[c % 2]
        cols = slice(c * w, (c + 1) * w)
        x = z[:, w:2 * w]
        head = xh_s[:, cols]
        xc = (cb_ref[:, cols] + cw_ref[3:4, cols] * x
              + cw_ref[2:3, cols] * _shift_rows(x, 1, head)
              + cw_ref[1:2, cols] * _shift_rows(x, 2, head)
              + cw_ref[0:1, cols] * _shift_rows(x, 3, head))
        xh_s[:, cols] = x[tt - SUBLANE:]
        pre = jnp.dot(xc.astype(BF16), wax_ref[c], preferred_element_type=F32)
        yield
        cv = z[:, 3 * w:4 * w] * z[:, 4 * w:5 * w]
        chead = cvh_s[:, cols]
        sc = z[:, 2 * w:3 * w] * (sw_ref[2:3, cols] * cv
                                  + sw_ref[1:2, cols] * _shift_rows(cv, 1, chead)
                                  + sw_ref[0:1, cols] * _shift_rows(cv, 2, chead))
        cvh_s[:, cols] = cv[tt - SUBLANE:]
        sc_ms = _group_mean_sq(sc, gmat)
        yield
        ga = _sigmoid(pre[:, :w] + ba_ref[:, cols])
        gx = _sigmoid(pre[:, w:] + bx_ref[:, cols])
        nl = -lam_ref[:, cols]
        sp = jnp.maximum(nl, 0.0) + jnp.log1p(jnp.exp(-jnp.abs(nl)))
        log_a = (-LRU_C * ga) * sp
        a = jnp.exp(log_a)
        mult = jnp.sqrt(-jnp.tanh(log_a) * (a * a + 1.0))
        u = mult * (gx * xc)
        yield
        a, u = _window_steps(a, u, (1, 2))
        yield
        a, u = _window_steps(a, u, (4,))
        h = _scan_groups(a, u, hc_s[:, cols])
        hc_s[:, cols] = h[tt - 1:]
        yield
        lo = h * _gelu_tanh(z[:, 0:w])
        lo_ms = _group_mean_sq(lo, gmat)
        yield
        m[:, w:] = (sc * lax.rsqrt(sc_ms + EPS) * sg_ref[:, cols]).astype(BF16)
        m[:, :w] = (lo * lax.rsqrt(lo_ms + EPS) * lg_ref[:, cols]).astype(BF16)
        yield

    for run in project_pieces(0):
        run()
    pending = []
    for c in range(nc):
        if c + 1 < nc:
            pending += project_pieces(c + 1)
        for _ in mixer(c):
            if pending:
                pending.pop(0)()
        while pending:
            pending.pop(0)()
        pending = outproj_pieces(c)
    for run in pending:
        run()

    o_ref[...] = h_ref[...] + _rms(o_ref[...], post_g_ref[...])
    xt_ref[...] = xh_s[...]
    cvt_ref[...] = cvh_s[...]
    hl_ref[0] = hc_s[...]


def _mix(h, p, xh0, cvh0, h0, *, nb, tt):
    rows, d = h.shape
    d_lru = p["cb"].shape[1]
    nt = rows // (nb * tt)
    assert rows == nb * nt * tt
    const = lambda b, t: (0, 0)
    once = pl.Buffered(1)
    vec = pl.BlockSpec((1, d_lru), const)
    gvec = pl.BlockSpec((1, d), const)
    state8 = jax.ShapeDtypeStruct((nb * SUBLANE, d_lru), F32)
    return pl.pallas_call(
        functools.partial(_mix_kernel, d_lru=d_lru),
        grid=(nb, nt),
        in_specs=[
            pl.BlockSpec((tt, d), lambda b, t: (b * nt + t, 0)),
            gvec,
            pl.BlockSpec(p["win"].shape, const, pipeline_mode=once),
            pl.BlockSpec(p["cw"].shape, const), vec,
            pl.BlockSpec(p["wax"].shape, lambda b, t: (0, 0, 0)),
            vec, vec, vec,
            pl.BlockSpec(p["sw"].shape, const), vec, vec,
            pl.BlockSpec((2 * MXU_DIM, MXU_DIM), const),
            pl.BlockSpec(p["wout"].shape, const, pipeline_mode=once),
            gvec,
            pl.BlockSpec((SUBLANE, d_lru), const),
            pl.BlockSpec((SUBLANE, d_lru), const),
            vec,
        ],
        out_specs=[
            pl.BlockSpec((tt, d), lambda b, t: (b * nt + t, 0)),
            pl.BlockSpec((SUBLANE, d_lru), lambda b, t: (b, 0)),
            pl.BlockSpec((SUBLANE, d_lru), lambda b, t: (b, 0)),
            pl.BlockSpec((1, 1, d_lru), lambda b, t: (b, 0, 0)),
        ],
        out_shape=[
            jax.ShapeDtypeStruct((rows, d), F32),
            state8, state8,
            jax.ShapeDtypeStruct((nb, 1, d_lru), F32),
        ],
        scratch_shapes=[
            pltpu.VMEM((SUBLANE, d_lru), F32),
            pltpu.VMEM((SUBLANE, d_lru), F32),
            pltpu.VMEM((1, d_lru), F32),
            pltpu.VMEM((tt, d), BF16),
            pltpu.VMEM((tt, N_Z * MXU_DIM), F32),
            pltpu.VMEM((tt, N_Z * MXU_DIM), F32),
            pltpu.VMEM((tt, 2 * MXU_DIM), BF16),
            pltpu.VMEM((tt, 2 * MXU_DIM), BF16),
        ],
        compiler_params=pltpu.CompilerParams(
            dimension_semantics=("parallel", "arbitrary"), vmem_limit_bytes=VMEM_LIMIT),
        name="mix",
    )(h, p["pre_g"], p["win"], p["cw"], p["cb"], p["wax"], p["ba"], p["bx"], p["lam"], p["sw"], p["lg"],
      p["sg"], p["gmat"], p["wout"], p["post_g"], xh0, cvh0, h0)


def _block_diag_gates(w_a, w_x):
    nh, blk, _ = w_a.shape
    per = MXU_DIM // blk
    eye = jnp.eye(per, dtype=w_a.dtype)

    def bd(w):
        w = w.reshape(nh // per, per, blk, blk)
        return jnp.einsum("cpij,pq->cpiqj", w, eye).reshape(nh // per, MXU_DIM, MXU_DIM)

    return jnp.concatenate([bd(w_a), bd(w_x)], axis=-1).astype(BF16)


def kernel(x, meta_tokens, ffn1_pre_g, ffn1_w_gate, ffn1_w_up, ffn1_w_down, ffn1_post_g, mix_pre_g, w_in,
           lru_conv_w, lru_conv_b, lru_w_a, lru_b_a, lru_w_x, lru_b_x, lru_lambda, sconv_w, lru_out_g,
           sconv_out_g, w_out, mix_post_g, ffn2_pre_g, ffn2_w_gate, ffn2_w_up, ffn2_w_down, ffn2_post_g):
    bn, seq, d = x.shape
    n_meta = meta_tokens.shape[0]
    d_lru = lru_conv_w.shape[-1]
    d_sc = sconv_w.shape[-1]
    assert ffn1_pre_g.shape[0] == 1, "single layer"
    assert n_meta % SUBLANE == 0 and n_meta >= SUBLANE
    assert d_lru == d_sc == N_LRU_HEADS * HEAD_DIM == N_SCONV_GROUPS * HEAD_DIM
    assert lru_w_a.shape[-1] == HEAD_DIM and d_lru % MXU_DIM == 0
    assert w_in.shape[-1] == N_Z * d_lru and w_out.shape[1] == 2 * d_lru

    w1 = tuple(w[0].astype(BF16) for w in (ffn1_w_gate, ffn1_w_up, ffn1_w_down))
    w2 = tuple(w[0].astype(BF16) for w in (ffn2_w_gate, ffn2_w_up, ffn2_w_down))
    head = jnp.arange(MXU_DIM) // HEAD_DIM
    mp = {
        "pre_g": mix_pre_g,
        "win": w_in[0].astype(BF16),
        "cw": lru_conv_w[0], "cb": lru_conv_b,
        "wax": _block_diag_gates(lru_w_a[0], lru_w_x[0]), "ba": lru_b_a, "bx": lru_b_x, "lam": lru_lambda,
        "sw": sconv_w[0], "lg": lru_out_g, "sg": sconv_out_g,
        "gmat": jnp.tile((head[:, None] == head[None, :]).astype(BF16), (2, 1)),
        "wout": w_out[0].reshape(2, d_lru // MXU_DIM, MXU_DIM, d).transpose(1, 0, 2, 3)
                        .reshape(2 * d_lru, d).astype(BF16),
        "post_g": mix_post_g,
    }
    ffn_tiles = dict(tm=1024, tf=768, edge_chunks=(512, 512))

    hm = _ffn(meta_tokens, ffn1_pre_g, *w1, ffn1_post_g, tm=n_meta, tf=ffn_tiles["tf"], edge_chunks=(n_meta,))
    zero8 = jnp.zeros((SUBLANE, d_lru), F32)
    _, xt, cvt, hl = _mix(hm, mp, zero8, zero8, jnp.zeros((1, d_lru), F32), nb=1, tt=n_meta)

    h = x.reshape(bn * seq, d)
    h = _ffn(h, ffn1_pre_g, *w1, ffn1_post_g, **ffn_tiles)
    h, _, _, _ = _mix(h, mp, xt, cvt, hl[0], nb=bn, tt=512)
    h = _ffn(h, ffn2_pre_g, *w2, ffn2_post_g, **ffn_tiles)
    return h.reshape(bn, seq, d)
```

```python
import functools

import jax
import jax.numpy as jnp
from jax import lax
from jax.experimental import pallas as pl
from jax.experimental.pallas import tpu as pltpu

EPS = 1e-6
N_LRU_HEADS = 16
N_SCONV_GROUPS = 16
HEAD_DIM = 64
LRU_C = 8.0
SUBLANE = 8
MXU_DIM = 256
TAIL = MXU_DIM // 2
VMEM_LIMIT = 56 * 1024 * 1024
N_Z = 5

F32 = jnp.float32
BF16 = jnp.bfloat16


def _sigmoid(x):
    return 0.5 + 0.5 * jnp.tanh(0.5 * x)


def _rms(x, g):
    ms = jnp.mean(x * x, axis=-1, keepdims=True)
    return x * lax.rsqrt(ms + EPS) * g


def _silu_mul(g, u):
    return (g * _sigmoid(g) * u).astype(BF16)


def _ffn_step(h_ref, pre_g_ref, wg_ref, wu_ref, wd_ref, wt_ref, post_g_ref, o_ref, xn_ref, *,
              first, last, width, chunks):
    r0 = 0
    for rc in chunks:
        rows = pl.ds(r0, rc)
        r0 += rc
        if first:
            xn = _rms(h_ref[rows, :], pre_g_ref[...]).astype(BF16)
            xn_ref[rows, :] = xn
        else:
            xn = xn_ref[rows, :]
        wm = width - TAIL if last else width
        g = jnp.dot(xn, wg_ref[:, :wm], preferred_element_type=F32)
        u = jnp.dot(xn, wu_ref[:, :wm], preferred_element_type=F32)
        acc = jnp.dot(_silu_mul(g, u), wd_ref[:wm, :], preferred_element_type=F32)
        if last:
            cuts = (0, rc // 2, rc) if rc >= 4 * SUBLANE else (0, rc)
            gu = jnp.concatenate([jnp.dot(xn[lo:hi], wt_ref[...], preferred_element_type=F32)
                                  for lo, hi in zip(cuts[:-1], cuts[1:])], axis=0)
            acc += jnp.dot(_silu_mul(gu[:, :TAIL], gu[:, TAIL:]), wd_ref[wm:width, :],
                           preferred_element_type=F32)
        if not first:
            acc = o_ref[rows, :] + acc
        o_ref[rows, :] = acc
        if last:
            o_ref[rows, :] = h_ref[rows, :] + _rms(o_ref[rows, :], 0.5 * post_g_ref[...])


def _ffn_kernel(h_ref, pre_g_ref, wg_ref, wu_ref, wd_ref, wt_ref, post_g_ref, o_ref, xn_ref, *,
                rem, edge_chunks):
    j = pl.program_id(1)
    nj = pl.num_programs(1)
    refs = (h_ref, pre_g_ref, wg_ref, wu_ref, wd_ref, wt_ref, post_g_ref, o_ref, xn_ref)
    tm = h_ref.shape[0]
    tf = wg_ref.shape[1]

    @pl.when(j == 0)
    def _():
        _ffn_step(*refs, first=True, last=False, width=tf, chunks=edge_chunks)

    @pl.when(jnp.logical_and(j > 0, j < nj - 1))
    def _():
        _ffn_step(*refs, first=False, last=False, width=tf, chunks=(tm,))

    @pl.when(j == nj - 1)
    def _():
        _ffn_step(*refs, first=False, last=True, width=rem, chunks=edge_chunks)


def _ffn(h, pre_g, wg, wu, wd, post_g, *, tm, tf, edge_chunks):
    m, d = h.shape
    f = wg.shape[1]
    nj = pl.cdiv(f, tf)
    rem = f - (nj - 1) * tf
    assert nj >= 2 and m % tm == 0 and sum(edge_chunks) == tm
    assert rem % MXU_DIM == TAIL, "d_ff ends in half an MXU tile"
    w_tail = jnp.concatenate([wg[:, f - TAIL:], wu[:, f - TAIL:]], axis=1)
    return pl.pallas_call(
        functools.partial(_ffn_kernel, rem=rem, edge_chunks=edge_chunks),
        grid=(m // tm, nj),
        in_specs=[
            pl.BlockSpec((tm, d), lambda i, j: (i, 0)),
            pl.BlockSpec((1, d), lambda i, j: (0, 0)),
            pl.BlockSpec((d, tf), lambda i, j: (0, j)),
            pl.BlockSpec((d, tf), lambda i, j: (0, j)),
            pl.BlockSpec((tf, d), lambda i, j: (j, 0)),
            pl.BlockSpec((d, 2 * TAIL), lambda i, j: (0, 0), pipeline_mode=pl.Buffered(1)),
            pl.BlockSpec((1, d), lambda i, j: (0, 0)),
        ],
        out_specs=pl.BlockSpec((tm, d), lambda i, j: (i, 0)),
        out_shape=jax.ShapeDtypeStruct((m, d), F32),
        scratch_shapes=[pltpu.VMEM((tm, d), BF16)],
        compiler_params=pltpu.CompilerParams(
            dimension_semantics=("parallel", "arbitrary"), vmem_limit_bytes=VMEM_LIMIT),
        name="ffn",
    )(h, pre_g, wg, wu, wd, w_tail, post_g)


def _shift_rows(x, k, head):
    r = pltpu.roll(x, k, 0)
    hr = pltpu.roll(head, k, 0)
    row = lax.broadcasted_iota(jnp.int32, head.shape, 0)
    first = jnp.where(row < k, hr, r[:SUBLANE])
    if x.shape[0] == SUBLANE:
        return first
    return jnp.concatenate([first, r[SUBLANE:]], axis=0)


def _shift_fill(x, s, fill):
    tt = x.shape[0]
    r = pltpu.roll(x, s, 0)
    row = lax.broadcasted_iota(jnp.int32, (SUBLANE,) + x.shape[1:], 0)
    first = jnp.where(row < s, fill, r[:SUBLANE])
    if tt == SUBLANE:
        return first
    return jnp.concatenate([first, r[SUBLANE:]], axis=0)


def _window_steps(a, u, steps):
    for s in steps:
        u = u + a * _shift_fill(u, s, 0.0)
        a = a * _shift_fill(a, s, 1.0)
    return a, u


def _scan_groups(a, u, h_in):
    hs = []
    h = h_in
    for g in range(a.shape[0] // SUBLANE):
        rows = slice(g * SUBLANE, (g + 1) * SUBLANE)
        h = a[rows] * h + u[rows]
        hs.append(h)
    return jnp.concatenate(hs, axis=0)


def _group_mean_sq(y, gmat):
    y2 = y * y
    hi = y2.astype(BF16)
    lo = (y2 - hi.astype(F32)).astype(BF16)
    s = jnp.dot(jnp.concatenate([hi, lo], axis=1), gmat, preferred_element_type=F32)
    return s * (1.0 / HEAD_DIM)


def _gelu_tanh(x):
    return 0.5 * x * (1.0 + jnp.tanh(0.7978845608028654 * (x + 0.044715 * (x * x * x))))


def _mix_kernel(h_ref, pre_g_ref, win_ref, cw_ref, cb_ref, wax_ref, ba_ref, bx_ref, lam_ref, sw_ref, lg_ref,
                sg_ref, gmat_ref, wout_ref, post_g_ref, xh0_ref, cvh0_ref, h0_ref,
                o_ref, xt_ref, cvt_ref, hl_ref,
                xh_s, cvh_s, hc_s, un_s, z0_s, z1_s, m0_s, m1_s, *, d_lru):
    @pl.when(pl.program_id(1) == 0)
    def _():
        xh_s[...] = xh0_ref[...]
        cvh_s[...] = cvh0_ref[...]
        hc_s[...] = h0_ref[...]

    tt, d = h_ref.shape
    w = MXU_DIM
    nc = d_lru // w
    z_slots = (z0_s, z1_s)
    m_slots = (m0_s, m1_s)
    gmat = gmat_ref[...]
    un_s[...] = _rms(h_ref[...], pre_g_ref[...]).astype(BF16)

    def project_pieces(c):
        def piece(k):
            def run():
                col = k * d_lru + c * w
                z_slots[c % 2][:, k * w:(k + 1) * w] = jnp.dot(
                    un_s[...], win_ref[:, col:col + w], preferred_element_type=F32)
            return run
        return [piece(k) for k in range(N_Z)]

    def outproj_pieces(c):
        half = d // 2

        def piece(n):
            def run():
                ncols = slice(n * half, (n + 1) * half)
                part = jnp.dot(m_slots[c % 2][...], wout_ref[c * 2 * w:(c + 1) * 2 * w, ncols],
                               preferred_element_type=F32)
                if c == 0:
                    o_ref[:, ncols] = part
                else:
                    o_ref[:, ncols] += part
            return run
        return [piece(0), piece(1)]

    def mixer(c):
        z = z_slots[c % 2]
        m = m_slots[c % 2]
        cols = slice(c * w, (c + 1) * w)
        x = z[:, w:2 * w]
        head = xh_s[:, cols]
        xc = (cb_ref[:, cols] + cw_ref[3:4, cols] * x
              + cw_ref[2:3, cols] * _shift_rows(x, 1, head)
              + cw_ref[1:2, cols] * _shift_rows(x, 2, head)
              + cw_ref[0:1, cols] * _shift_rows(x, 3, head))
        xh_s[:, cols] = x[tt - SUBLANE:]
        pre = jnp.dot(xc.astype(BF16), wax_ref[c], preferred_element_type=F32)
        yield
        cv = z[:, 3 * w:4 * w] * z[:, 4 * w:5 * w]
        chead = cvh_s[:, cols]
        sc = z[:, 2 * w:3 * w] * (sw_ref[2:3, cols] * cv
                                  + sw_ref[1:2, cols] * _shift_rows(cv, 1, chead)
                                  + sw_ref[0:1, cols] * _shift_rows(cv, 2, chead))
        cvh_s[:, cols] = cv[tt - SUBLANE:]
        sc_ms = _group_mean_sq(sc, gmat)
        yield
        ga = _sigmoid(pre[:, :w] + ba_ref[:, cols])
        gx = _sigmoid(pre[:, w:] + bx_ref[:, cols])
        nl = -lam_ref[:, cols]
        sp = jnp.maximum(nl, 0.0) + jnp.log1p(jnp.exp(-jnp.abs(nl)))
        log_a = (-LRU_C * ga) * sp
        a = jnp.exp(log_a)
        mult = jnp.sqrt(-jnp.tanh(log_a) * (a * a + 1.0))
        u = mult * (gx * xc)
        yield
        a, u = _window_steps(a, u, (1, 2))
        yield
        a, u = _window_steps(a, u, (4,))
        h = _scan_groups(a, u, hc_s[:, cols])
        hc_s[:, cols] = h[tt - 1:]
        yield
        lo = h * _gelu_tanh(z[:, 0:w])
        lo_ms = _group_mean_sq(lo, gmat)
        yield
        m[:, w:] = (sc * lax.rsqrt(sc_ms + EPS) * sg_ref[:, cols]).astype(BF16)
        m[:, :w] = (lo * lax.rsqrt(lo_ms + EPS) * lg_ref[:, cols]).astype(BF16)
        yield

    for run in project_pieces(0):
        run()
    pending = []
    for c in range(nc):
        if c + 1 < nc:
            pending += project_pieces(c + 1)
        for _ in mixer(c):
            if pending:
                pending.pop(0)()
        while pending:
            pending.pop(0)()
        pending = outproj_pieces(c)
    for run in pending:
        run()

    o_ref[...] = h_ref[...] + _rms(o_ref[...], post_g_ref[...])
    xt_ref[...] = xh_s[...]
    cvt_ref[...] = cvh_s[...]
    hl_ref[0] = hc_s[...]


def _mix(h, p, xh0, cvh0, h0, *, nb, tt):
    rows, d = h.shape
    d_lru = p["cb"].shape[1]
    nt = rows // (nb * tt)
    assert rows == nb * nt * tt
    const = lambda b, t: (0, 0)
    once = pl.Buffered(1)
    vec = pl.BlockSpec((1, d_lru), const)
    gvec = pl.BlockSpec((1, d), const)
    state8 = jax.ShapeDtypeStruct((nb * SUBLANE, d_lru), F32)
    return pl.pallas_call(
        functools.partial(_mix_kernel, d_lru=d_lru),
        grid=(nb, nt),
        in_specs=[
            pl.BlockSpec((tt, d), lambda b, t: (b * nt + t, 0)),
            gvec,
            pl.BlockSpec(p["win"].shape, const, pipeline_mode=once),
            pl.BlockSpec(p["cw"].shape, const), vec,
            pl.BlockSpec(p["wax"].shape, lambda b, t: (0, 0, 0)),
            vec, vec, vec,
            pl.BlockSpec(p["sw"].shape, const), vec, vec,
            pl.BlockSpec((2 * MXU_DIM, MXU_DIM), const),
            pl.BlockSpec(p["wout"].shape, const, pipeline_mode=once),
            gvec,
            pl.BlockSpec((SUBLANE, d_lru), const),
            pl.BlockSpec((SUBLANE, d_lru), const),
            vec,
        ],
        out_specs=[
            pl.BlockSpec((tt, d), lambda b, t: (b * nt + t, 0)),
            pl.BlockSpec((SUBLANE, d_lru), lambda b, t: (b, 0)),
            pl.BlockSpec((SUBLANE, d_lru), lambda b, t: (b, 0)),
            pl.BlockSpec((1, 1, d_lru), lambda b, t: (b, 0, 0)),
        ],
        out_shape=[
            jax.ShapeDtypeStruct((rows, d), F32),
            state8, state8,
            jax.ShapeDtypeStruct((nb, 1, d_lru), F32),
        ],
        scratch_shapes=[
            pltpu.VMEM((SUBLANE, d_lru), F32),
            pltpu.VMEM((SUBLANE, d_lru), F32),
            pltpu.VMEM((1, d_lru), F32),
            pltpu.VMEM((tt, d), BF16),
            pltpu.VMEM((tt, N_Z * MXU_DIM), F32),
            pltpu.VMEM((tt, N_Z * MXU_DIM), F32),
            pltpu.VMEM((tt, 2 * MXU_DIM), BF16),
            pltpu.VMEM((tt, 2 * MXU_DIM), BF16),
        ],
        compiler_params=pltpu.CompilerParams(
            dimension_semantics=("parallel", "arbitrary"), vmem_limit_bytes=VMEM_LIMIT),
        name="mix",
    )(h, p["pre_g"], p["win"], p["cw"], p["cb"], p["wax"], p["ba"], p["bx"], p["lam"], p["sw"], p["lg"],
      p["sg"], p["gmat"], p["wout"], p["post_g"], xh0, cvh0, h0)


def _block_diag_gates(w_a, w_x):
    nh, blk, _ = w_a.shape
    per = MXU_DIM // blk
    eye = jnp.eye(per, dtype=w_a.dtype)

    def bd(w):
        w = w.reshape(nh // per, per, blk, blk)
        return jnp.einsum("cpij,pq->cpiqj", w, eye).reshape(nh // per, MXU_DIM, MXU_DIM)

    return jnp.concatenate([bd(w_a), bd(w_x)], axis=-1).astype(BF16)


def kernel(x, meta_tokens, ffn1_pre_g, ffn1_w_gate, ffn1_w_up, ffn1_w_down, ffn1_post_g, mix_pre_g, w_in,
           lru_conv_w, lru_conv_b, lru_w_a, lru_b_a, lru_w_x, lru_b_x, lru_lambda, sconv_w, lru_out_g,
           sconv_out_g, w_out, mix_post_g, ffn2_pre_g, ffn2_w_gate, ffn2_w_up, ffn2_w_down, ffn2_post_g):
    bn, seq, d = x.shape
    n_meta = meta_tokens.shape[0]
    d_lru = lru_conv_w.shape[-1]
    d_sc = sconv_w.shape[-1]
    assert ffn1_pre_g.shape[0] == 1, "single layer"
    assert n_meta % SUBLANE == 0 and n_meta >= SUBLANE
    assert d_lru == d_sc == N_LRU_HEADS * HEAD_DIM == N_SCONV_GROUPS * HEAD_DIM
    assert lru_w_a.shape[-1] == HEAD_DIM and d_lru % MXU_DIM == 0
    assert w_in.shape[-1] == N_Z * d_lru and w_out.shape[1] == 2 * d_lru

    w1 = tuple(w[0].astype(BF16) for w in (ffn1_w_gate, ffn1_w_up, ffn1_w_down))
    w2 = tuple(w[0].astype(BF16) for w in (ffn2_w_gate, ffn2_w_up, ffn2_w_down))
    head = jnp.arange(MXU_DIM) // HEAD_DIM
    mp = {
        "pre_g": mix_pre_g,
        "win": w_in[0].astype(BF16),
        "cw": lru_conv_w[0], "cb": lru_conv_b,
        "wax": _block_diag_gates(lru_w_a[0], lru_w_x[0]), "ba": lru_b_a, "bx": lru_b_x, "lam": lru_lambda,
        "sw": sconv_w[0], "lg": lru_out_g, "sg": sconv_out_g,
        "gmat": jnp.tile((head[:, None] == head[None, :]).astype(BF16), (2, 1)),
        "wout": w_out[0].reshape(2, d_lru // MXU_DIM, MXU_DIM, d).transpose(1, 0, 2, 3)
                        .reshape(2 * d_lru, d).astype(BF16),
        "post_g": mix_post_g,
    }
    ffn_tiles = dict(tm=1024, tf=512, edge_chunks=(512, 512))

    hm = _ffn(meta_tokens, ffn1_pre_g, *w1, ffn1_post_g, tm=n_meta, tf=ffn_tiles["tf"], edge_chunks=(n_meta,))
    zero8 = jnp.zeros((SUBLANE, d_lru), F32)
    _, xt, cvt, hl = _mix(hm, mp, zero8, zero8, jnp.zeros((1, d_lru), F32), nb=1, tt=n_meta)

    h = x.reshape(bn * seq, d)
    h = _ffn(h, ffn1_pre_g, *w1, ffn1_post_g, **ffn_tiles)
    h, _, _, _ = _mix(h, mp, xt, cvt, hl[0], nb=bn, tt=512)
    h = _ffn(h, ffn2_pre_g, *w2, ffn2_post_g, **ffn_tiles)
    return h.reshape(bn, seq, d)
```

```python
import functools

import jax
import jax.numpy as jnp
from jax import lax
from jax.experimental import pallas as pl
from jax.experimental.pallas import tpu as pltpu

EPS = 1e-6
N_LRU_HEADS = 16
N_SCONV_GROUPS = 16
HEAD_DIM = 64
LRU_C = 8.0
SUBLANE = 8
MXU_DIM = 256
TAIL = MXU_DIM // 2
VMEM_LIMIT = 56 * 1024 * 1024
N_Z = 5

F32 = jnp.float32
BF16 = jnp.bfloat16


def _sigmoid(x):
    return 0.5 + 0.5 * jnp.tanh(0.5 * x)


def _rms(x, g):
    ms = jnp.mean(x * x, axis=-1, keepdims=True)
    return x * lax.rsqrt(ms + EPS) * g


def _silu_mul(g, u):
    return (g * _sigmoid(g) * u).astype(BF16)


def _ffn_step(h_ref, pre_g_ref, wg_ref, wu_ref, wd_ref, wt_ref, post_g_ref, o_ref, xn_ref, *,
              first, last, width, chunks):
    r0 = 0
    for rc in chunks:
        rows = pl.ds(r0, rc)
        r0 += rc
        if first:
            xn = _rms(h_ref[rows, :], pre_g_ref[...]).astype(BF16)
            xn_ref[rows, :] = xn
        else:
            xn = xn_ref[rows, :]
        wm = width - TAIL if last else width
        g = jnp.dot(xn, wg_ref[:, :wm], preferred_element_type=F32)
        u = jnp.dot(xn, wu_ref[:, :wm], preferred_element_type=F32)
        acc = jnp.dot(_silu_mul(g, u), wd_ref[:wm, :], preferred_element_type=F32)
        if last:
            cuts = (0, rc // 2, rc) if rc >= 4 * SUBLANE else (0, rc)
            gu = jnp.concatenate([jnp.dot(xn[lo:hi], wt_ref[...], preferred_element_type=F32)
                                  for lo, hi in zip(cuts[:-1], cuts[1:])], axis=0)
            acc += jnp.dot(_silu_mul(gu[:, :TAIL], gu[:, TAIL:]), wd_ref[wm:width, :],
                           preferred_element_type=F32)
        if not first:
            acc = o_ref[rows, :] + acc
        o_ref[rows, :] = acc
        if last:
            o_ref[rows, :] = h_ref[rows, :] + _rms(o_ref[rows, :], 0.5 * post_g_ref[...])


def _ffn_kernel(h_ref, pre_g_ref, wg_ref, wu_ref, wd_ref, wt_ref, post_g_ref, o_ref, xn_ref, *,
                rem, edge_chunks):
    j = pl.program_id(1)
    nj = pl.num_programs(1)
    refs = (h_ref, pre_g_ref, wg_ref, wu_ref, wd_ref, wt_ref, post_g_ref, o_ref, xn_ref)
    tm = h_ref.shape[0]
    tf = wg_ref.shape[1]

    @pl.when(j == 0)
    def _():
        _ffn_step(*refs, first=True, last=False, width=tf, chunks=edge_chunks)

    @pl.when(jnp.logical_and(j > 0, j < nj - 1))
    def _():
        _ffn_step(*refs, first=False, last=False, width=tf, chunks=(tm,))

    @pl.when(j == nj - 1)
    def _():
        _ffn_step(*refs, first=False, last=True, width=rem, chunks=edge_chunks)


def _ffn(h, pre_g, wg, wu, wd, post_g, *, tm, tf, edge_chunks):
    m, d = h.shape
    f = wg.shape[1]
    nj = pl.cdiv(f, tf)
    rem = f - (nj - 1) * tf
    assert nj >= 2 and m % tm == 0 and sum(edge_chunks) == tm
    assert rem % MXU_DIM == TAIL, "d_ff ends in half an MXU tile"
    w_tail = jnp.concatenate([wg[:, f - TAIL:], wu[:, f - TAIL:]], axis=1)
    return pl.pallas_call(
        functools.partial(_ffn_kernel, rem=rem, edge_chunks=edge_chunks),
        grid=(m // tm, nj),
        in_specs=[
            pl.BlockSpec((tm, d), lambda i, j: (i, 0)),
            pl.BlockSpec((1, d), lambda i, j: (0, 0)),
            pl.BlockSpec((d, tf), lambda i, j: (0, j)),
            pl.BlockSpec((d, tf), lambda i, j: (0, j)),
            pl.BlockSpec((tf, d), lambda i, j: (j, 0)),
            pl.BlockSpec((d, 2 * TAIL), lambda i, j: (0, 0), pipeline_mode=pl.Buffered(1)),
            pl.BlockSpec((1, d), lambda i, j: (0, 0)),
        ],
        out_specs=pl.BlockSpec((tm, d), lambda i, j: (i, 0)),
        out_shape=jax.ShapeDtypeStruct((m, d), F32),
        scratch_shapes=[pltpu.VMEM((tm, d), BF16)],
        compiler_params=pltpu.CompilerParams(
            dimension_semantics=("parallel", "arbitrary"), vmem_limit_bytes=VMEM_LIMIT),
        name="ffn",
    )(h, pre_g, wg, wu, wd, w_tail, post_g)


def _shift_rows(x, k, head):
    r = pltpu.roll(x, k, 0)
    hr = pltpu.roll(head, k, 0)
    row = lax.broadcasted_iota(jnp.int32, head.shape, 0)
    first = jnp.where(row < k, hr, r[:SUBLANE])
    if x.shape[0] == SUBLANE:
        return first
    return jnp.concatenate([first, r[SUBLANE:]], axis=0)


def _shift_fill(x, s, fill):
    tt = x.shape[0]
    r = pltpu.roll(x, s, 0)
    row = lax.broadcasted_iota(jnp.int32, (SUBLANE,) + x.shape[1:], 0)
    first = jnp.where(row < s, fill, r[:SUBLANE])
    if tt == SUBLANE:
        return first
    return jnp.concatenate([first, r[SUBLANE:]], axis=0)


def _window_steps(a, u, steps):
    for s in steps:
        u = u + a * _shift_fill(u, s, 0.0)
        a = a * _shift_fill(a, s, 1.0)
    return a, u


def _scan_groups(a, u, h_in):
    hs = []
    h = h_in
    for g in range(a.shape[0] // SUBLANE):
        rows = slice(g * SUBLANE, (g + 1) * SUBLANE)
        h = a[rows] * h + u[rows]
        hs.append(h)
    return jnp.concatenate(hs, axis=0)


def _group_mean_sq(y, gmat):
    y2 = y * y
    hi = y2.astype(BF16)
    lo = (y2 - hi.astype(F32)).astype(BF16)
    s = jnp.dot(jnp.concatenate([hi, lo], axis=1), gmat, preferred_element_type=F32)
    return s * (1.0 / HEAD_DIM)


def _gelu_tanh(x):
    return 0.5 * x * (1.0 + jnp.tanh(0.7978845608028654 * (x + 0.044715 * (x * x * x))))


def _mix_kernel(h_ref, pre_g_ref, win_ref, cw_ref, cb_ref, wax_ref, ba_ref, bx_ref, lam_ref, sw_ref, lg_ref,
                sg_ref, gmat_ref, wout_ref, post_g_ref, xh0_ref, cvh0_ref, h0_ref,
                o_ref, xt_ref, cvt_ref, hl_ref,
                xh_s, cvh_s, hc_s, un_s, z0_s, z1_s, m0_s, m1_s, *, d_lru):
    @pl.when(pl.program_id(1) == 0)
    def _():
        xh_s[...] = xh0_ref[...]
        cvh_s[...] = cvh0_ref[...]
        hc_s[...] = h0_ref[...]

    tt, d = h_ref.shape
    w = MXU_DIM
    nc = d_lru // w
    z_slots = (z0_s, z1_s)
    m_slots = (m0_s, m1_s)
    split_last = tt >= 4 * SUBLANE
    gmat = gmat_ref[...]
    un_s[...] = _rms(h_ref[...], pre_g_ref[...]).astype(BF16)

    def project_pieces(c):
        def piece(k):
            def run():
                col = k * d_lru + c * w
                z_slots[c % 2][:, k * w:(k + 1) * w] = jnp.dot(
                    un_s[...], win_ref[:, col:col + w], preferred_element_type=F32)
            return run
        return [piece(k) for k in range(N_Z)]

    def outproj_pieces(c):
        half = d // 2
        wrows = slice(c * 2 * w, (c + 1) * 2 * w)

        def piece(n):
            def run():
                ncols = slice(n * half, (n + 1) * half)
                part = jnp.dot(m_slots[c % 2][...], wout_ref[wrows, ncols], preferred_element_type=F32)
                if c == 0:
                    o_ref[:, ncols] = part
                else:
                    o_ref[:, ncols] += part
            return run

        def last_piece(r):
            def run():
                rows = pl.ds(r * (tt // 2), tt // 2)
                acc = o_ref[rows, :] + jnp.dot(m_slots[c % 2][rows, :], wout_ref[wrows, :],
                                               preferred_element_type=F32)
                o_ref[rows, :] = h_ref[rows, :] + _rms(acc, post_g_ref[...])
            return run

        if c == nc - 1 and split_last:
            return [last_piece(0), last_piece(1)]
        return [piece(0), piece(1)]

    def mixer(c):
        z = z_slots[c % 2]
        m = m_slots[c % 2]
        cols = slice(c * w, (c + 1) * w)
        x = z[:, w:2 * w]
        head = xh_s[:, cols]
        xc = (cb_ref[:, cols] + cw_ref[3:4, cols] * x
              + cw_ref[2:3, cols] * _shift_rows(x, 1, head)
              + cw_ref[1:2, cols] * _shift_rows(x, 2, head)
              + cw_ref[0:1, cols] * _shift_rows(x, 3, head))
        xh_s[:, cols] = x[tt - SUBLANE:]
        pre = jnp.dot(xc.astype(BF16), wax_ref[c], preferred_element_type=F32)
        yield
        cv = z[:, 3 * w:4 * w] * z[:, 4 * w:5 * w]
        chead = cvh_s[:, cols]
        sc = z[:, 2 * w:3 * w] * (sw_ref[2:3, cols] * cv
                                  + sw_ref[1:2, cols] * _shift_rows(cv, 1, chead)
                                  + sw_ref[0:1, cols] * _shift_rows(cv, 2, chead))
        cvh_s[:, cols] = cv[tt - SUBLANE:]
        sc_ms = _group_mean_sq(sc, gmat)
        yield
        ga = _sigmoid(pre[:, :w] + ba_ref[:, cols])
        gx = _sigmoid(pre[:, w:] + bx_ref[:, cols])
        nl = -lam_ref[:, cols]
        sp = jnp.maximum(nl, 0.0) + jnp.log1p(jnp.exp(-jnp.abs(nl)))
        log_a = (-LRU_C * ga) * sp
        a = jnp.exp(log_a)
        m2 = -jnp.tanh(log_a) * (a * a + 1.0)
        mult = jnp.where(m2 > 0.0, m2 * lax.rsqrt(m2), 0.0)
        u = mult * (gx * xc)
        yield
        a, u = _window_steps(a, u, (1, 2))
        yield
        a, u = _window_steps(a, u, (4,))
        h = _scan_groups(a, u, hc_s[:, cols])
        hc_s[:, cols] = h[tt - 1:]
        yield
        lo = h * _gelu_tanh(z[:, 0:w])
        lo_ms = _group_mean_sq(lo, gmat)
        yield
        m[:, w:] = (sc * lax.rsqrt(sc_ms + EPS) * sg_ref[:, cols]).astype(BF16)
        m[:, :w] = (lo * lax.rsqrt(lo_ms + EPS) * lg_ref[:, cols]).astype(BF16)
        yield

    for run in project_pieces(0):
        run()
    pending = []
    for c in range(nc):
        if c + 1 < nc:
            pending += project_pieces(c + 1)
        for _ in mixer(c):
            if pending:
                pending.pop(0)()
        while pending:
            pending.pop(0)()
        pending = outproj_pieces(c)
    for run in pending:
        run()

    if not split_last:
        o_ref[...] = h_ref[...] + _rms(o_ref[...], post_g_ref[...])
    xt_ref[...] = xh_s[...]
    cvt_ref[...] = cvh_s[...]
    hl_ref[0] = hc_s[...]


def _mix(h, p, xh0, cvh0, h0, *, nb, tt):
    rows, d = h.shape
    d_lru = p["cb"].shape[1]
    nt = rows // (nb * tt)
    assert rows == nb * nt * tt
    const = lambda b, t: (0, 0)
    once = pl.Buffered(1)
    vec = pl.BlockSpec((1, d_lru), const)
    gvec = pl.BlockSpec((1, d), const)
    state8 = jax.ShapeDtypeStruct((nb * SUBLANE, d_lru), F32)
    return pl.pallas_call(
        functools.partial(_mix_kernel, d_lru=d_lru),
        grid=(nb, nt),
        in_specs=[
            pl.BlockSpec((tt, d), lambda b, t: (b * nt + t, 0)),
            gvec,
            pl.BlockSpec(p["win"].shape, const, pipeline_mode=once),
            pl.BlockSpec(p["cw"].shape, const), vec,
            pl.BlockSpec(p["wax"].shape, lambda b, t: (0, 0, 0)),
            vec, vec, vec,
            pl.BlockSpec(p["sw"].shape, const), vec, vec,
            pl.BlockSpec((2 * MXU_DIM, MXU_DIM), const),
            pl.BlockSpec(p["wout"].shape, const, pipeline_mode=once),
            gvec,
            pl.BlockSpec((SUBLANE, d_lru), const),
            pl.BlockSpec((SUBLANE, d_lru), const),
            vec,
        ],
        out_specs=[
            pl.BlockSpec((tt, d), lambda b, t: (b * nt + t, 0)),
            pl.BlockSpec((SUBLANE, d_lru), lambda b, t: (b, 0)),
            pl.BlockSpec((SUBLANE, d_lru), lambda b, t: (b, 0)),
            pl.BlockSpec((1, 1, d_lru), lambda b, t: (b, 0, 0)),
        ],
        out_shape=[
            jax.ShapeDtypeStruct((rows, d), F32),
            state8, state8,
            jax.ShapeDtypeStruct((nb, 1, d_lru), F32),
        ],
        scratch_shapes=[
            pltpu.VMEM((SUBLANE, d_lru), F32),
            pltpu.VMEM((SUBLANE, d_lru), F32),
            pltpu.VMEM((1, d_lru), F32),
            pltpu.VMEM((tt, d), BF16),
            pltpu.VMEM((tt, N_Z * MXU_DIM), F32),
            pltpu.VMEM((tt, N_Z * MXU_DIM), F32),
            pltpu.VMEM((tt, 2 * MXU_DIM), BF16),
            pltpu.VMEM((tt, 2 * MXU_DIM), BF16),
        ],
        compiler_params=pltpu.CompilerParams(
            dimension_semantics=("parallel", "arbitrary"), vmem_limit_bytes=VMEM_LIMIT),
        name="mix",
    )(h, p["pre_g"], p["win"], p["cw"], p["cb"], p["wax"], p["ba"], p["bx"], p["lam"], p["sw"], p["lg"],
      p["sg"], p["gmat"], p["wout"], p["post_g"], xh0, cvh0, h0)


def _block_diag_gates(w_a, w_x):
    nh, blk, _ = w_a.shape
    per = MXU_DIM // blk
    eye = jnp.eye(per, dtype=w_a.dtype)

    def bd(w):
        w = w.reshape(nh // per, per, blk, blk)
        return jnp.einsum("cpij,pq->cpiqj", w, eye).reshape(nh // per, MXU_DIM, MXU_DIM)

    return jnp.concatenate([bd(w_a), bd(w_x)], axis=-1).astype(BF16)


def kernel(x, meta_tokens, ffn1_pre_g, ffn1_w_gate, ffn1_w_up, ffn1_w_down, ffn1_post_g, mix_pre_g, w_in,
           lru_conv_w, lru_conv_b, lru_w_a, lru_b_a, lru_w_x, lru_b_x, lru_lambda, sconv_w, lru_out_g,
           sconv_out_g, w_out, mix_post_g, ffn2_pre_g, ffn2_w_gate, ffn2_w_up, ffn2_w_down, ffn2_post_g):
    bn, seq, d = x.shape
    n_meta = meta_tokens.shape[0]
    d_lru = lru_conv_w.shape[-1]
    d_sc = sconv_w.shape[-1]
    assert ffn1_pre_g.shape[0] == 1, "single layer"
    assert n_meta % SUBLANE == 0 and n_meta >= SUBLANE
    assert d_lru == d_sc == N_LRU_HEADS * HEAD_DIM == N_SCONV_GROUPS * HEAD_DIM
    assert lru_w_a.shape[-1] == HEAD_DIM and d_lru % MXU_DIM == 0
    assert w_in.shape[-1] == N_Z * d_lru and w_out.shape[1] == 2 * d_lru

    w1 = tuple(w[0].astype(BF16) for w in (ffn1_w_gate, ffn1_w_up, ffn1_w_down))
    w2 = tuple(w[0].astype(BF16) for w in (ffn2_w_gate, ffn2_w_up, ffn2_w_down))
    head = jnp.arange(MXU_DIM) // HEAD_DIM
    mp = {
        "pre_g": mix_pre_g,
        "win": w_in[0].astype(BF16),
        "cw": lru_conv_w[0], "cb": lru_conv_b,
        "wax": _block_diag_gates(lru_w_a[0], lru_w_x[0]), "ba": lru_b_a, "bx": lru_b_x, "lam": lru_lambda,
        "sw": sconv_w[0], "lg": lru_out_g, "sg": sconv_out_g,
        "gmat": jnp.tile((head[:, None] == head[None, :]).astype(BF16), (2, 1)),
        "wout": w_out[0].reshape(2, d_lru // MXU_DIM, MXU_DIM, d).transpose(1, 0, 2, 3)
                        .reshape(2 * d_lru, d).astype(BF16),
        "post_g": mix_post_g,
    }
    ffn_tiles = dict(tm=1024, tf=512, edge_chunks=(512, 512))

    hm = _ffn(meta_tokens, ffn1_pre_g, *w1, ffn1_post_g, tm=n_meta, tf=ffn_tiles["tf"], edge_chunks=(n_meta,))
    zero8 = jnp.zeros((SUBLANE, d_lru), F32)
    _, xt, cvt, hl = _mix(hm, mp, zero8, zero8, jnp.zeros((1, d_lru), F32), nb=1, tt=n_meta)

    h = x.reshape(bn * seq, d)
    h = _ffn(h, ffn1_pre_g, *w1, ffn1_post_g, **ffn_tiles)
    h, _, _, _ = _mix(h, mp, xt, cvt, hl[0], nb=bn, tt=512)
    h = _ffn(h, ffn2_pre_g, *w2, ffn2_post_g, **ffn_tiles)
    return h.reshape(bn, seq, d)
```

```python
import functools

import jax
import jax.numpy as jnp
from jax import lax
from jax.experimental import pallas as pl
from jax.experimental.pallas import tpu as pltpu

EPS = 1e-6
N_LRU_HEADS = 16
N_SCONV_GROUPS = 16
HEAD_DIM = 64
LRU_C = 8.0
SUBLANE = 8
BF16_ROWS = 2 * SUBLANE
MXU_DIM = 256
TAIL = MXU_DIM // 2
VMEM_LIMIT = 56 * 1024 * 1024
N_Z = 5

F32 = jnp.float32
BF16 = jnp.bfloat16


def _sigmoid(x):
    return 0.5 + 0.5 * jnp.tanh(0.5 * x)


def _rms(x, g):
    ms = jnp.mean(x * x, axis=-1, keepdims=True)
    return x * lax.rsqrt(ms + EPS) * g


def _silu_mul(g, u):
    return (g * _sigmoid(g) * u).astype(BF16)


def _ffn_step(h_ref, pre_g_ref, wg_ref, wu_ref, wd_ref, wt_ref, post_g_ref, o_ref, xn_ref, *,
              first, last, width, chunks):
    r0 = 0
    for rc in chunks:
        rows = pl.ds(r0, rc)
        r0 += rc
        if first:
            xn = _rms(h_ref[rows, :], pre_g_ref[...]).astype(BF16)
            xn_ref[rows, :] = xn
        else:
            xn = xn_ref[rows, :]
        wm = width - TAIL if last else width
        g = jnp.dot(xn, wg_ref[:, :wm], preferred_element_type=F32)
        u = jnp.dot(xn, wu_ref[:, :wm], preferred_element_type=F32)
        acc = jnp.dot(_silu_mul(g, u), wd_ref[:wm, :], preferred_element_type=F32)
        if last:
            cuts = (0, rc // 2, rc) if rc >= 2 * BF16_ROWS else (0, rc)
            gu = jnp.concatenate([jnp.dot(xn[lo:hi], wt_ref[...], preferred_element_type=F32)
                                  for lo, hi in zip(cuts[:-1], cuts[1:])], axis=0)
            acc += jnp.dot(_silu_mul(gu[:, :TAIL], gu[:, TAIL:]), wd_ref[wm:width, :],
                           preferred_element_type=F32)
        if not first:
            acc = o_ref[rows, :] + acc
        o_ref[rows, :] = acc
        if last:
            o_ref[rows, :] = h_ref[rows, :] + _rms(o_ref[rows, :], 0.5 * post_g_ref[...])


def _ffn_kernel(h_ref, pre_g_ref, wg_ref, wu_ref, wd_ref, wt_ref, post_g_ref, o_ref, xn_ref, *,
                rem, edge_chunks):
    j = pl.program_id(1)
    nj = pl.num_programs(1)
    refs = (h_ref, pre_g_ref, wg_ref, wu_ref, wd_ref, wt_ref, post_g_ref, o_ref, xn_ref)
    tm = h_ref.shape[0]
    tf = wg_ref.shape[1]

    @pl.when(j == 0)
    def _():
        _ffn_step(*refs, first=True, last=False, width=tf, chunks=edge_chunks)

    @pl.when(jnp.logical_and(j > 0, j < nj - 1))
    def _():
        _ffn_step(*refs, first=False, last=False, width=tf, chunks=(tm,))

    @pl.when(j == nj - 1)
    def _():
        _ffn_step(*refs, first=False, last=True, width=rem, chunks=edge_chunks)


def _ffn(h, pre_g, wg, wu, wd, post_g, *, tm, tf, edge_chunks):
    m, d = h.shape
    f = wg.shape[1]
    nj = pl.cdiv(f, tf)
    rem = f - (nj - 1) * tf
    assert nj >= 2 and m % tm == 0 and sum(edge_chunks) == tm
    assert rem % MXU_DIM == TAIL, "d_ff ends in half an MXU tile"
    w_tail = jnp.concatenate([wg[:, f - TAIL:], wu[:, f - TAIL:]], axis=1)
    return pl.pallas_call(
        functools.partial(_ffn_kernel, rem=rem, edge_chunks=edge_chunks),
        grid=(m // tm, nj),
        in_specs=[
            pl.BlockSpec((tm, d), lambda i, j: (i, 0)),
            pl.BlockSpec((1, d), lambda i, j: (0, 0)),
            pl.BlockSpec((d, tf), lambda i, j: (0, j)),
            pl.BlockSpec((d, tf), lambda i, j: (0, j)),
            pl.BlockSpec((tf, d), lambda i, j: (j, 0)),
            pl.BlockSpec((d, 2 * TAIL), lambda i, j: (0, 0), pipeline_mode=pl.Buffered(1)),
            pl.BlockSpec((1, d), lambda i, j: (0, 0)),
        ],
        out_specs=pl.BlockSpec((tm, d), lambda i, j: (i, 0)),
        out_shape=jax.ShapeDtypeStruct((m, d), F32),
        scratch_shapes=[pltpu.VMEM((tm, d), BF16)],
        compiler_params=pltpu.CompilerParams(
            dimension_semantics=("parallel", "arbitrary"), vmem_limit_bytes=VMEM_LIMIT),
        name="ffn",
    )(h, pre_g, wg, wu, wd, w_tail, post_g)


def _shift_rows(x, k, head):
    r = pltpu.roll(x, k, 0)
    hr = pltpu.roll(head, k, 0)
    row = lax.broadcasted_iota(jnp.int32, head.shape, 0)
    first = jnp.where(row < k, hr, r[:SUBLANE])
    if x.shape[0] == SUBLANE:
        return first
    return jnp.concatenate([first, r[SUBLANE:]], axis=0)


def _shift_fill(x, s, fill):
    tt = x.shape[0]
    r = pltpu.roll(x, s, 0)
    row = lax.broadcasted_iota(jnp.int32, (SUBLANE,) + x.shape[1:], 0)
    first = jnp.where(row < s, fill, r[:SUBLANE])
    if tt == SUBLANE:
        return first
    return jnp.concatenate([first, r[SUBLANE:]], axis=0)


def _window_steps(a, u, steps):
    for s in steps:
        u = u + a * _shift_fill(u, s, 0.0)
        a = a * _shift_fill(a, s, 1.0)
    return a, u


def _scan_groups(a, u, h_in):
    hs = []
    h = h_in
    for g in range(a.shape[0] // SUBLANE):
        rows = slice(g * SUBLANE, (g + 1) * SUBLANE)
        h = a[rows] * h + u[rows]
        hs.append(h)
    return jnp.concatenate(hs, axis=0)


def _group_mean_sq(y, gmat):
    y2 = y * y
    hi = y2.astype(BF16)
    lo = (y2 - hi.astype(F32)).astype(BF16)
    s = jnp.dot(jnp.concatenate([hi, lo], axis=1), gmat, preferred_element_type=F32)
    return s * (1.0 / HEAD_DIM)


def _gelu_tanh(x):
    return 0.5 * x * (1.0 + jnp.tanh(0.7978845608028654 * (x + 0.044715 * (x * x * x))))


def _mix_kernel(h_ref, pre_g_ref, win_ref, cw_ref, cb_ref, wax_ref, ba_ref, bx_ref, lam_ref, sw_ref, lg_ref,
                sg_ref, gmat_ref, wout_ref, post_g_ref, xh0_ref, cvh0_ref, h0_ref,
                o_ref, xt_ref, cvt_ref, hl_ref,
                xh_s, cvh_s, hc_s, un_s, z0_s, z1_s, m0_s, m1_s, *, d_lru):
    @pl.when(pl.program_id(1) == 0)
    def _():
        xh_s[...] = xh0_ref[...]
        cvh_s[...] = cvh0_ref[...]
        hc_s[...] = h0_ref[...]

    tt, d = h_ref.shape
    w = MXU_DIM
    nc = d_lru // w
    z_slots = (z0_s, z1_s)
    m_slots = (m0_s, m1_s)
    split_last = tt >= 2 * BF16_ROWS
    gmat = gmat_ref[...]
    un_s[...] = _rms(h_ref[...], pre_g_ref[...]).astype(BF16)

    def project_pieces(c):
        def piece(k):
            def run():
                col = k * d_lru + c * w
                z_slots[c % 2][:, k * w:(k + 1) * w] = jnp.dot(
                    un_s[...], win_ref[:, col:col + w], preferred_element_type=F32)
            return run
        return [piece(k) for k in range(N_Z)]

    def outproj_pieces(c):
        half = d // 2
        wrows = slice(c * 2 * w, (c + 1) * 2 * w)

        def piece(n):
            def run():
                ncols = slice(n * half, (n + 1) * half)
                part = jnp.dot(m_slots[c % 2][...], wout_ref[wrows, ncols], preferred_element_type=F32)
                if c == 0:
                    o_ref[:, ncols] = part
                else:
                    o_ref[:, ncols] += part
            return run

        def last_piece(r):
            def run():
                rows = pl.ds(r * (tt // 2), tt // 2)
                acc = jnp.dot(m_slots[c % 2][rows, :], wout_ref[wrows, :], preferred_element_type=F32)
                if c > 0:
                    acc = o_ref[rows, :] + acc
                o_ref[rows, :] = h_ref[rows, :] + _rms(acc, post_g_ref[...])
            return run

        if c == nc - 1 and split_last:
            return [last_piece(0), last_piece(1)]
        return [piece(0), piece(1)]

    def mixer(c):
        z = z_slots[c % 2]
        m = m_slots[c % 2]
        cols = slice(c * w, (c + 1) * w)
        x = z[:, w:2 * w]
        head = xh_s[:, cols]
        xc = (cb_ref[:, cols] + cw_ref[3:4, cols] * x
              + cw_ref[2:3, cols] * _shift_rows(x, 1, head)
              + cw_ref[1:2, cols] * _shift_rows(x, 2, head)
              + cw_ref[0:1, cols] * _shift_rows(x, 3, head))
        xh_s[:, cols] = x[tt - SUBLANE:]
        pre = jnp.dot(xc.astype(BF16), wax_ref[c], preferred_element_type=F32)
        yield
        cv = z[:, 3 * w:4 * w] * z[:, 4 * w:5 * w]
        chead = cvh_s[:, cols]
        sc = z[:, 2 * w:3 * w] * (sw_ref[2:3, cols] * cv
                                  + sw_ref[1:2, cols] * _shift_rows(cv, 1, chead)
                                  + sw_ref[0:1, cols] * _shift_rows(cv, 2, chead))
        cvh_s[:, cols] = cv[tt - SUBLANE:]
        sc_ms = _group_mean_sq(sc, gmat)
        yield
        ga = _sigmoid(pre[:, :w] + ba_ref[:, cols])
        gx = _sigmoid(pre[:, w:] + bx_ref[:, cols])
        nl = -lam_ref[:, cols]
        sp = jnp.maximum(nl, 0.0) + jnp.log1p(jnp.exp(-jnp.abs(nl)))
        log_a = (-LRU_C * ga) * sp
        a = jnp.exp(log_a)
        m2 = -jnp.tanh(log_a) * (a * a + 1.0)
        mult = jnp.where(m2 > 0.0, m2 * lax.rsqrt(m2), 0.0)
        u = mult * (gx * xc)
        yield
        a, u = _window_steps(a, u, (1, 2))
        yield
        a, u = _window_steps(a, u, (4,))
        h = _scan_groups(a, u, hc_s[:, cols])
        hc_s[:, cols] = h[tt - 1:]
        yield
        lo = h * _gelu_tanh(z[:, 0:w])
        lo_ms = _group_mean_sq(lo, gmat)
        yield
        m[:, w:] = (sc * lax.rsqrt(sc_ms + EPS) * sg_ref[:, cols]).astype(BF16)
        m[:, :w] = (lo * lax.rsqrt(lo_ms + EPS) * lg_ref[:, cols]).astype(BF16)
        yield

    for run in project_pieces(0):
        run()
    pending = []
    for c in range(nc):
        if c + 1 < nc:
            pending += project_pieces(c + 1)
        for _ in mixer(c):
            if pending:
                pending.pop(0)()
        while pending:
            pending.pop(0)()
        pending = outproj_pieces(c)
    for run in pending:
        run()

    if not split_last:
        o_ref[...] = h_ref[...] + _rms(o_ref[...], post_g_ref[...])
    xt_ref[...] = xh_s[...]
    cvt_ref[...] = cvh_s[...]
    hl_ref[0] = hc_s[...]


def _mix(h, p, xh0, cvh0, h0, *, nb, tt):
    rows, d = h.shape
    d_lru = p["cb"].shape[1]
    nt = rows // (nb * tt)
    assert rows == nb * nt * tt
    const = lambda b, t: (0, 0)
    once = pl.Buffered(1)
    vec = pl.BlockSpec((1, d_lru), const)
    gvec = pl.BlockSpec((1, d), const)
    state8 = jax.ShapeDtypeStruct((nb * SUBLANE, d_lru), F32)
    return pl.pallas_call(
        functools.partial(_mix_kernel, d_lru=d_lru),
        grid=(nb, nt),
        in_specs=[
            pl.BlockSpec((tt, d), lambda b, t: (b * nt + t, 0)),
            gvec,
            pl.BlockSpec(p["win"].shape, const, pipeline_mode=once),
            pl.BlockSpec(p["cw"].shape, const), vec,
            pl.BlockSpec(p["wax"].shape, lambda b, t: (0, 0, 0)),
            vec, vec, vec,
            pl.BlockSpec(p["sw"].shape, const), vec, vec,
            pl.BlockSpec((2 * MXU_DIM, MXU_DIM), const),
            pl.BlockSpec(p["wout"].shape, const, pipeline_mode=once),
            gvec,
            pl.BlockSpec((SUBLANE, d_lru), const),
            pl.BlockSpec((SUBLANE, d_lru), const),
            vec,
        ],
        out_specs=[
            pl.BlockSpec((tt, d), lambda b, t: (b * nt + t, 0)),
            pl.BlockSpec((SUBLANE, d_lru), lambda b, t: (b, 0)),
            pl.BlockSpec((SUBLANE, d_lru), lambda b, t: (b, 0)),
            pl.BlockSpec((1, 1, d_lru), lambda b, t: (b, 0, 0)),
        ],
        out_shape=[
            jax.ShapeDtypeStruct((rows, d), F32),
            state8, state8,
            jax.ShapeDtypeStruct((nb, 1, d_lru), F32),
        ],
        scratch_shapes=[
            pltpu.VMEM((SUBLANE, d_lru), F32),
            pltpu.VMEM((SUBLANE, d_lru), F32),
            pltpu.VMEM((1, d_lru), F32),
            pltpu.VMEM((tt, d), BF16),
            pltpu.VMEM((tt, N_Z * MXU_DIM), F32),
            pltpu.VMEM((tt, N_Z * MXU_DIM), F32),
            pltpu.VMEM((tt, 2 * MXU_DIM), BF16),
            pltpu.VMEM((tt, 2 * MXU_DIM), BF16),
        ],
        compiler_params=pltpu.CompilerParams(
            dimension_semantics=("parallel", "arbitrary"), vmem_limit_bytes=VMEM_LIMIT),
        name="mix",
    )(h, p["pre_g"], p["win"], p["cw"], p["cb"], p["wax"], p["ba"], p["bx"], p["lam"], p["sw"], p["lg"],
      p["sg"], p["gmat"], p["wout"], p["post_g"], xh0, cvh0, h0)


def _block_diag_gates(w_a, w_x):
    nh, blk, _ = w_a.shape
    per = MXU_DIM // blk
    eye = jnp.eye(per, dtype=w_a.dtype)

    def bd(w):
        w = w.reshape(nh // per, per, blk, blk)
        return jnp.einsum("cpij,pq->cpiqj", w, eye).reshape(nh // per, MXU_DIM, MXU_DIM)

    return jnp.concatenate([bd(w_a), bd(w_x)], axis=-1).astype(BF16)


def kernel(x, meta_tokens, ffn1_pre_g, ffn1_w_gate, ffn1_w_up, ffn1_w_down, ffn1_post_g, mix_pre_g, w_in,
           lru_conv_w, lru_conv_b, lru_w_a, lru_b_a, lru_w_x, lru_b_x, lru_lambda, sconv_w, lru_out_g,
           sconv_out_g, w_out, mix_post_g, ffn2_pre_g, ffn2_w_gate, ffn2_w_up, ffn2_w_down, ffn2_post_g):
    bn, seq, d = x.shape
    n_meta = meta_tokens.shape[0]
    d_lru = lru_conv_w.shape[-1]
    d_sc = sconv_w.shape[-1]
    assert ffn1_pre_g.shape[0] == 1, "single layer"
    assert n_meta % SUBLANE == 0 and n_meta >= SUBLANE
    assert d_lru == d_sc == N_LRU_HEADS * HEAD_DIM == N_SCONV_GROUPS * HEAD_DIM
    assert lru_w_a.shape[-1] == HEAD_DIM and d_lru % MXU_DIM == 0
    assert w_in.shape[-1] == N_Z * d_lru and w_out.shape[1] == 2 * d_lru

    w1 = tuple(w[0].astype(BF16) for w in (ffn1_w_gate, ffn1_w_up, ffn1_w_down))
    w2 = tuple(w[0].astype(BF16) for w in (ffn2_w_gate, ffn2_w_up, ffn2_w_down))
    head = jnp.arange(MXU_DIM) // HEAD_DIM
    mp = {
        "pre_g": mix_pre_g,
        "win": w_in[0].astype(BF16),
        "cw": lru_conv_w[0], "cb": lru_conv_b,
        "wax": _block_diag_gates(lru_w_a[0], lru_w_x[0]), "ba": lru_b_a, "bx": lru_b_x, "lam": lru_lambda,
        "sw": sconv_w[0], "lg": lru_out_g, "sg": sconv_out_g,
        "gmat": jnp.tile((head[:, None] == head[None, :]).astype(BF16), (2, 1)),
        "wout": w_out[0].reshape(2, d_lru // MXU_DIM, MXU_DIM, d).transpose(1, 0, 2, 3)
                        .reshape(2 * d_lru, d).astype(BF16),
        "post_g": mix_post_g,
    }
    ffn_tiles = dict(tm=1024, tf=512, edge_chunks=(512, 512))

    hm = _ffn(meta_tokens, ffn1_pre_g, *w1, ffn1_post_g, tm=n_meta, tf=ffn_tiles["tf"], edge_chunks=(n_meta,))
    zero8 = jnp.zeros((SUBLANE, d_lru), F32)
    _, xt, cvt, hl = _mix(hm, mp, zero8, zero8, jnp.zeros((1, d_lru), F32), nb=1, tt=n_meta)

    h = x.reshape(bn * seq, d)
    h = _ffn(h, ffn1_pre_g, *w1, ffn1_post_g, **ffn_tiles)
    h, _, _, _ = _mix(h, mp, xt, cvt, hl[0], nb=bn, tt=512)
    h = _ffn(h, ffn2_pre_g, *w2, ffn2_post_g, **ffn_tiles)
    return h.reshape(bn, seq, d)
```

```python
import functools

import jax
import jax.numpy as jnp
from jax import lax
from jax.experimental import pallas as pl
from jax.experimental.pallas import tpu as pltpu

EPS = 1e-6
N_LRU_HEADS = 16
N_SCONV_GROUPS = 16
HEAD_DIM = 64
LRU_C = 8.0
SUBLANE = 8
BF16_ROWS = 2 * SUBLANE
MXU_DIM = 256
TAIL = MXU_DIM // 2
VMEM_LIMIT = 56 * 1024 * 1024
N_Z = 5

F32 = jnp.float32
BF16 = jnp.bfloat16


def _sigmoid(x):
    return 0.5 + 0.5 * jnp.tanh(0.5 * x)


def _rms(x, g):
    ms = jnp.mean(x * x, axis=-1, keepdims=True)
    return x * lax.rsqrt(ms + EPS) * g


def _silu_mul(g, u):
    return (g * _sigmoid(g) * u).astype(BF16)


def _ffn_step(h_ref, pre_g_ref, wg_ref, wu_ref, wd_ref, wt_ref, post_g_ref, o_ref, xn_ref, *,
              first, last, width, chunks):
    r0 = 0
    for rc in chunks:
        rows = pl.ds(r0, rc)
        r0 += rc
        if first:
            xn = _rms(h_ref[rows, :], pre_g_ref[...]).astype(BF16)
            xn_ref[rows, :] = xn
        else:
            xn = xn_ref[rows, :]
        wm = width - TAIL if last else width
        g = jnp.dot(xn, wg_ref[:, :wm], preferred_element_type=F32)
        u = jnp.dot(xn, wu_ref[:, :wm], preferred_element_type=F32)
        acc = jnp.dot(_silu_mul(g, u), wd_ref[:wm, :], preferred_element_type=F32)
        if last:
            cuts = (0, rc // 2, rc) if rc >= 2 * BF16_ROWS else (0, rc)
            gu = jnp.concatenate([jnp.dot(xn[lo:hi], wt_ref[...], preferred_element_type=F32)
                                  for lo, hi in zip(cuts[:-1], cuts[1:])], axis=0)
            acc += jnp.dot(_silu_mul(gu[:, :TAIL], gu[:, TAIL:]), wd_ref[wm:width, :],
                           preferred_element_type=F32)
        if not first:
            acc = o_ref[rows, :] + acc
        o_ref[rows, :] = acc
        if last:
            o_ref[rows, :] = h_ref[rows, :] + _rms(o_ref[rows, :], 0.5 * post_g_ref[...])


def _ffn_kernel(h_ref, pre_g_ref, wg_ref, wu_ref, wd_ref, wt_ref, post_g_ref, o_ref, xn_ref, *,
                rem, edge_chunks):
    j = pl.program_id(1)
    nj = pl.num_programs(1)
    refs = (h_ref, pre_g_ref, wg_ref, wu_ref, wd_ref, wt_ref, post_g_ref, o_ref, xn_ref)
    tm = h_ref.shape[0]
    tf = wg_ref.shape[1]

    @pl.when(j == 0)
    def _():
        _ffn_step(*refs, first=True, last=False, width=tf, chunks=edge_chunks)

    @pl.when(jnp.logical_and(j > 0, j < nj - 1))
    def _():
        _ffn_step(*refs, first=False, last=False, width=tf, chunks=(tm,))

    @pl.when(j == nj - 1)
    def _():
        _ffn_step(*refs, first=False, last=True, width=rem, chunks=edge_chunks)


def _ffn(h, pre_g, wg, wu, wd, post_g, *, tm, tf, edge_chunks):
    m, d = h.shape
    f = wg.shape[1]
    nj = pl.cdiv(f, tf)
    rem = f - (nj - 1) * tf
    assert nj >= 2 and m % tm == 0 and sum(edge_chunks) == tm
    assert rem % MXU_DIM == TAIL, "d_ff ends in half an MXU tile"
    w_tail = jnp.concatenate([wg[:, f - TAIL:], wu[:, f - TAIL:]], axis=1)
    return pl.pallas_call(
        functools.partial(_ffn_kernel, rem=rem, edge_chunks=edge_chunks),
        grid=(m // tm, nj),
        in_specs=[
            pl.BlockSpec((tm, d), lambda i, j: (i, 0)),
            pl.BlockSpec((1, d), lambda i, j: (0, 0)),
            pl.BlockSpec((d, tf), lambda i, j: (0, j)),
            pl.BlockSpec((d, tf), lambda i, j: (0, j)),
            pl.BlockSpec((tf, d), lambda i, j: (j, 0)),
            pl.BlockSpec((d, 2 * TAIL), lambda i, j: (0, 0), pipeline_mode=pl.Buffered(1)),
            pl.BlockSpec((1, d), lambda i, j: (0, 0)),
        ],
        out_specs=pl.BlockSpec((tm, d), lambda i, j: (i, 0)),
        out_shape=jax.ShapeDtypeStruct((m, d), F32),
        scratch_shapes=[pltpu.VMEM((tm, d), BF16)],
        compiler_params=pltpu.CompilerParams(
            dimension_semantics=("parallel", "arbitrary"), vmem_limit_bytes=VMEM_LIMIT),
        name="ffn",
    )(h, pre_g, wg, wu, wd, w_tail, post_g)


def _shift_rows(x, k, head):
    r = pltpu.roll(x, k, 0)
    hr = pltpu.roll(head, k, 0)
    row = lax.broadcasted_iota(jnp.int32, head.shape, 0)
    first = jnp.where(row < k, hr, r[:SUBLANE])
    if x.shape[0] == SUBLANE:
        return first
    return jnp.concatenate([first, r[SUBLANE:]], axis=0)


def _shift_fill(x, s, fill):
    tt = x.shape[0]
    r = pltpu.roll(x, s, 0)
    row = lax.broadcasted_iota(jnp.int32, (SUBLANE,) + x.shape[1:], 0)
    first = jnp.where(row < s, fill, r[:SUBLANE])
    if tt == SUBLANE:
        return first
    return jnp.concatenate([first, r[SUBLANE:]], axis=0)


def _window_steps(a, u, steps):
    for s in steps:
        u = u + a * _shift_fill(u, s, 0.0)
        a = a * _shift_fill(a, s, 1.0)
    return a, u


def _scan_groups(a, u, h_in):
    hs = []
    h = h_in
    for g in range(a.shape[0] // SUBLANE):
        rows = slice(g * SUBLANE, (g + 1) * SUBLANE)
        h = a[rows] * h + u[rows]
        hs.append(h)
    return jnp.concatenate(hs, axis=0)


def _group_mean_sq(y, gmat):
    y2 = y * y
    hi = y2.astype(BF16)
    lo = (y2 - hi.astype(F32)).astype(BF16)
    return jnp.dot(jnp.concatenate([hi, lo], axis=1), gmat, preferred_element_type=F32)


def _gelu_tanh(x):
    return 0.5 * x * (1.0 + jnp.tanh(0.7978845608028654 * (x + 0.044715 * (x * x * x))))


def _mix_kernel(h_ref, pre_g_ref, win_ref, cw_ref, cb_ref, wax_ref, ba_ref, bx_ref, lam_ref, sw_ref, lg_ref,
                sg_ref, gmat_ref, wout_ref, post_g_ref, xh0_ref, cvh0_ref, h0_ref,
                o_ref, xt_ref, cvt_ref, hl_ref,
                xh_s, cvh_s, hc_s, un_s, z0_s, z1_s, m0_s, m1_s, *, d_lru):
    @pl.when(pl.program_id(1) == 0)
    def _():
        xh_s[...] = xh0_ref[...]
        cvh_s[...] = cvh0_ref[...]
        hc_s[...] = h0_ref[...]

    tt, d = h_ref.shape
    w = MXU_DIM
    nc = d_lru // w
    z_slots = (z0_s, z1_s)
    m_slots = (m0_s, m1_s)
    split_last = tt >= 2 * BF16_ROWS
    gmat = gmat_ref[...]
    un_s[...] = _rms(h_ref[...], pre_g_ref[...]).astype(BF16)

    def project_pieces(c):
        def piece(k):
            def run():
                col = k * d_lru + c * w
                z_slots[c % 2][:, k * w:(k + 1) * w] = jnp.dot(
                    un_s[...], win_ref[:, col:col + w], preferred_element_type=F32)
            return run
        return [piece(k) for k in range(N_Z)]

    def outproj_pieces(c):
        half = d // 2
        wrows = slice(c * 2 * w, (c + 1) * 2 * w)

        def piece(n):
            def run():
                ncols = slice(n * half, (n + 1) * half)
                part = jnp.dot(m_slots[c % 2][...], wout_ref[wrows, ncols], preferred_element_type=F32)
                if c == 0:
                    o_ref[:, ncols] = part
                else:
                    o_ref[:, ncols] += part
            return run

        def last_piece(r):
            def run():
                rows = pl.ds(r * (tt // 2), tt // 2)
                acc = jnp.dot(m_slots[c % 2][rows, :], wout_ref[wrows, :], preferred_element_type=F32)
                if c > 0:
                    acc = o_ref[rows, :] + acc
                o_ref[rows, :] = h_ref[rows, :] + _rms(acc, post_g_ref[...])
            return run

        if c == nc - 1 and split_last:
            return [last_piece(0), last_piece(1)]
        return [piece(0), piece(1)]

    def mixer(c):
        z = z_slots[c % 2]
        m = m_slots[c % 2]
        cols = slice(c * w, (c + 1) * w)
        x = z[:, w:2 * w]
        head = xh_s[:, cols]
        xc = (cb_ref[:, cols] + cw_ref[3:4, cols] * x
              + cw_ref[2:3, cols] * _shift_rows(x, 1, head)
              + cw_ref[1:2, cols] * _shift_rows(x, 2, head)
              + cw_ref[0:1, cols] * _shift_rows(x, 3, head))
        xh_s[:, cols] = x[tt - SUBLANE:]
        pre = jnp.dot(xc.astype(BF16), wax_ref[c], preferred_element_type=F32)
        yield
        cv = z[:, 3 * w:4 * w] * z[:, 4 * w:5 * w]
        chead = cvh_s[:, cols]
        sc = z[:, 2 * w:3 * w] * (sw_ref[2:3, cols] * cv
                                  + sw_ref[1:2, cols] * _shift_rows(cv, 1, chead)
                                  + sw_ref[0:1, cols] * _shift_rows(cv, 2, chead))
        cvh_s[:, cols] = cv[tt - SUBLANE:]
        sc_ms = _group_mean_sq(sc, gmat)
        yield
        ga = 0.5 + 0.5 * jnp.tanh(pre[:, :w] + 0.5 * ba_ref[:, cols])
        gx = 0.5 + 0.5 * jnp.tanh(pre[:, w:] + 0.5 * bx_ref[:, cols])
        nl = -lam_ref[:, cols]
        sp = jnp.maximum(nl, 0.0) + jnp.log1p(jnp.exp(-jnp.abs(nl)))
        log_a = ga * (-LRU_C * sp)
        a = jnp.exp(log_a)
        m2 = -jnp.tanh(log_a) * (a * a + 1.0)
        mult = jnp.where(m2 > 0.0, m2 * lax.rsqrt(m2), 0.0)
        u = mult * (gx * xc)
        yield
        a, u = _window_steps(a, u, (1, 2))
        yield
        a, u = _window_steps(a, u, (4,))
        h = _scan_groups(a, u, hc_s[:, cols])
        hc_s[:, cols] = h[tt - 1:]
        yield
        lo = h * _gelu_tanh(z[:, 0:w])
        lo_ms = _group_mean_sq(lo, gmat)
        yield
        m[:, w:] = (sc * lax.rsqrt(sc_ms + EPS) * sg_ref[:, cols]).astype(BF16)
        m[:, :w] = (lo * lax.rsqrt(lo_ms + EPS) * lg_ref[:, cols]).astype(BF16)
        yield

    for run in project_pieces(0):
        run()
    pending = []
    for c in range(nc):
        if c + 1 < nc:
            pending += project_pieces(c + 1)
        for _ in mixer(c):
            if pending:
                pending.pop(0)()
        while pending:
            pending.pop(0)()
        pending = outproj_pieces(c)
    for run in pending:
        run()

    if not split_last:
        o_ref[...] = h_ref[...] + _rms(o_ref[...], post_g_ref[...])
    xt_ref[...] = xh_s[...]
    cvt_ref[...] = cvh_s[...]
    hl_ref[0] = hc_s[...]


def _mix(h, p, xh0, cvh0, h0, *, nb, tt):
    rows, d = h.shape
    d_lru = p["cb"].shape[1]
    nt = rows // (nb * tt)
    assert rows == nb * nt * tt
    const = lambda b, t: (0, 0)
    once = pl.Buffered(1)
    vec = pl.BlockSpec((1, d_lru), const)
    gvec = pl.BlockSpec((1, d), const)
    state8 = jax.ShapeDtypeStruct((nb * SUBLANE, d_lru), F32)
    return pl.pallas_call(
        functools.partial(_mix_kernel, d_lru=d_lru),
        grid=(nb, nt),
        in_specs=[
            pl.BlockSpec((tt, d), lambda b, t: (b * nt + t, 0)),
            gvec,
            pl.BlockSpec(p["win"].shape, const, pipeline_mode=once),
            pl.BlockSpec(p["cw"].shape, const), vec,
            pl.BlockSpec(p["wax"].shape, lambda b, t: (0, 0, 0)),
            vec, vec, vec,
            pl.BlockSpec(p["sw"].shape, const), vec, vec,
            pl.BlockSpec((2 * MXU_DIM, MXU_DIM), const),
            pl.BlockSpec(p["wout"].shape, const, pipeline_mode=once),
            gvec,
            pl.BlockSpec((SUBLANE, d_lru), const),
            pl.BlockSpec((SUBLANE, d_lru), const),
            vec,
        ],
        out_specs=[
            pl.BlockSpec((tt, d), lambda b, t: (b * nt + t, 0)),
            pl.BlockSpec((SUBLANE, d_lru), lambda b, t: (b, 0)),
            pl.BlockSpec((SUBLANE, d_lru), lambda b, t: (b, 0)),
            pl.BlockSpec((1, 1, d_lru), lambda b, t: (b, 0, 0)),
        ],
        out_shape=[
            jax.ShapeDtypeStruct((rows, d), F32),
            state8, state8,
            jax.ShapeDtypeStruct((nb, 1, d_lru), F32),
        ],
        scratch_shapes=[
            pltpu.VMEM((SUBLANE, d_lru), F32),
            pltpu.VMEM((SUBLANE, d_lru), F32),
            pltpu.VMEM((1, d_lru), F32),
            pltpu.VMEM((tt, d), BF16),
            pltpu.VMEM((tt, N_Z * MXU_DIM), F32),
            pltpu.VMEM((tt, N_Z * MXU_DIM), F32),
            pltpu.VMEM((tt, 2 * MXU_DIM), BF16),
            pltpu.VMEM((tt, 2 * MXU_DIM), BF16),
        ],
        compiler_params=pltpu.CompilerParams(
            dimension_semantics=("parallel", "arbitrary"), vmem_limit_bytes=VMEM_LIMIT),
        name="mix",
    )(h, p["pre_g"], p["win"], p["cw"], p["cb"], p["wax"], p["ba"], p["bx"], p["lam"], p["sw"], p["lg"],
      p["sg"], p["gmat"], p["wout"], p["post_g"], xh0, cvh0, h0)


def _block_diag_gates(w_a, w_x):
    nh, blk, _ = w_a.shape
    per = MXU_DIM // blk
    eye = jnp.eye(per, dtype=w_a.dtype)

    def bd(w):
        w = w.reshape(nh // per, per, blk, blk)
        return jnp.einsum("cpij,pq->cpiqj", w, eye).reshape(nh // per, MXU_DIM, MXU_DIM)

    return jnp.concatenate([bd(w_a), bd(w_x)], axis=-1).astype(BF16)


def kernel(x, meta_tokens, ffn1_pre_g, ffn1_w_gate, ffn1_w_up, ffn1_w_down, ffn1_post_g, mix_pre_g, w_in,
           lru_conv_w, lru_conv_b, lru_w_a, lru_b_a, lru_w_x, lru_b_x, lru_lambda, sconv_w, lru_out_g,
           sconv_out_g, w_out, mix_post_g, ffn2_pre_g, ffn2_w_gate, ffn2_w_up, ffn2_w_down, ffn2_post_g):
    bn, seq, d = x.shape
    n_meta = meta_tokens.shape[0]
    d_lru = lru_conv_w.shape[-1]
    d_sc = sconv_w.shape[-1]
    assert ffn1_pre_g.shape[0] == 1, "single layer"
    assert n_meta % SUBLANE == 0 and n_meta >= SUBLANE
    assert d_lru == d_sc == N_LRU_HEADS * HEAD_DIM == N_SCONV_GROUPS * HEAD_DIM
    assert lru_w_a.shape[-1] == HEAD_DIM and d_lru % MXU_DIM == 0
    assert w_in.shape[-1] == N_Z * d_lru and w_out.shape[1] == 2 * d_lru

    w1 = tuple(w[0].astype(BF16) for w in (ffn1_w_gate, ffn1_w_up, ffn1_w_down))
    w2 = tuple(w[0].astype(BF16) for w in (ffn2_w_gate, ffn2_w_up, ffn2_w_down))
    head = jnp.arange(MXU_DIM) // HEAD_DIM
    mp = {
        "pre_g": mix_pre_g,
        "win": w_in[0].astype(BF16),
        "cw": lru_conv_w[0], "cb": lru_conv_b,
        "wax": _block_diag_gates(0.5 * lru_w_a[0], 0.5 * lru_w_x[0]),
        "ba": lru_b_a, "bx": lru_b_x, "lam": lru_lambda,
        "sw": sconv_w[0], "lg": lru_out_g, "sg": sconv_out_g,
        "gmat": jnp.tile(((head[:, None] == head[None, :]).astype(F32) / HEAD_DIM).astype(BF16), (2, 1)),
        "wout": w_out[0].reshape(2, d_lru // MXU_DIM, MXU_DIM, d).transpose(1, 0, 2, 3)
                        .reshape(2 * d_lru, d).astype(BF16),
        "post_g": mix_post_g,
    }
    ffn_tiles = dict(tm=1024, tf=512, edge_chunks=(512, 512))

    hm = _ffn(meta_tokens, ffn1_pre_g, *w1, ffn1_post_g, tm=n_meta, tf=ffn_tiles["tf"], edge_chunks=(n_meta,))
    zero8 = jnp.zeros((SUBLANE, d_lru), F32)
    _, xt, cvt, hl = _mix(hm, mp, zero8, zero8, jnp.zeros((1, d_lru), F32), nb=1, tt=n_meta)

    h = x.reshape(bn * seq, d)
    h = _ffn(h, ffn1_pre_g, *w1, ffn1_post_g, **ffn_tiles)
    h, _, _, _ = _mix(h, mp, xt, cvt, hl[0], nb=bn, tt=512)
    h = _ffn(h, ffn2_pre_g, *w2, ffn2_post_g, **ffn_tiles)
    return h.reshape(bn, seq, d)
```

```python
import functools

import jax
import jax.numpy as jnp
from jax import lax
from jax.experimental import pallas as pl
from jax.experimental.pallas import tpu as pltpu

EPS = 1e-6
N_LRU_HEADS = 16
N_SCONV_GROUPS = 16
HEAD_DIM = 64
LRU_C = 8.0
SUBLANE = 8
BF16_ROWS = 2 * SUBLANE
MXU_DIM = 256
TAIL = MXU_DIM // 2
VMEM_LIMIT = 56 * 1024 * 1024
N_Z = 5

F32 = jnp.float32
BF16 = jnp.bfloat16


def _sigmoid(x):
    return 0.5 + 0.5 * jnp.tanh(0.5 * x)


def _rms(x, g):
    ms = jnp.mean(x * x, axis=-1, keepdims=True)
    return x * lax.rsqrt(ms + EPS) * g


def _silu_mul(g, u):
    return (g * _sigmoid(g) * u).astype(BF16)


def _ffn_step(h_ref, e_ref, pre_g_ref, wg_ref, wu_ref, wd_ref, wt_ref, post_g_ref, o_ref, eo_ref, xn_ref, *,
              first, last, width, chunks, extra):
    tm = h_ref.shape[0]
    r0 = 0
    for ci, rc in enumerate(chunks):
        rows = pl.ds(r0, rc)
        ride = e_ref.shape[0] if extra and ci == len(chunks) - 1 else 0
        if first:
            xn_ref[rows, :] = _rms(h_ref[rows, :], pre_g_ref[...]).astype(BF16)
            if ride:
                xn_ref[pl.ds(tm, ride), :] = _rms(e_ref[...], pre_g_ref[...]).astype(BF16)
        nr = rc + ride
        xn = xn_ref[pl.ds(r0, nr), :]
        wm = width - TAIL if last else width
        g = jnp.dot(xn, wg_ref[:, :wm], preferred_element_type=F32)
        u = jnp.dot(xn, wu_ref[:, :wm], preferred_element_type=F32)
        acc = jnp.dot(_silu_mul(g, u), wd_ref[:wm, :], preferred_element_type=F32)
        if last:
            cuts = (0, rc // 2, nr) if rc >= 2 * BF16_ROWS else (0, nr)
            gu = jnp.concatenate([jnp.dot(xn[lo:hi], wt_ref[...], preferred_element_type=F32)
                                  for lo, hi in zip(cuts[:-1], cuts[1:])], axis=0)
            acc += jnp.dot(_silu_mul(gu[:, :TAIL], gu[:, TAIL:]), wd_ref[wm:width, :],
                           preferred_element_type=F32)
        main = acc[:rc]
        if not first:
            main = o_ref[rows, :] + main
        o_ref[rows, :] = main
        if last:
            o_ref[rows, :] = h_ref[rows, :] + _rms(o_ref[rows, :], 0.5 * post_g_ref[...])
        if ride:
            ext = acc[rc:]
            if not first:
                ext = eo_ref[...] + ext
            if last:
                ext = e_ref[...] + _rms(ext, 0.5 * post_g_ref[...])
            eo_ref[...] = ext
        r0 += rc


def _ffn_kernel(*refs, rem, edge_chunks, has_extra):
    if has_extra:
        h_ref, e_ref, pre_g_ref, wg_ref, wu_ref, wd_ref, wt_ref, post_g_ref, o_ref, eo_ref, xn_ref = refs
    else:
        h_ref, pre_g_ref, wg_ref, wu_ref, wd_ref, wt_ref, post_g_ref, o_ref, xn_ref = refs
        e_ref = eo_ref = None
    i = pl.program_id(0)
    j = pl.program_id(1)
    nj = pl.num_programs(1)
    tm = h_ref.shape[0]
    tf = wg_ref.shape[1]

    def step(**kw):
        _ffn_step(h_ref, e_ref, pre_g_ref, wg_ref, wu_ref, wd_ref, wt_ref, post_g_ref, o_ref, eo_ref, xn_ref, **kw)

    def steps(tile_cond, extra):
        @pl.when(jnp.logical_and(tile_cond, j == 0))
        def _():
            step(first=True, last=False, width=tf, chunks=edge_chunks, extra=extra)

        @pl.when(jnp.logical_and(tile_cond, jnp.logical_and(j > 0, j < nj - 1)))
        def _():
            step(first=False, last=False, width=tf, chunks=(tm,), extra=extra)

        @pl.when(jnp.logical_and(tile_cond, j == nj - 1))
        def _():
            step(first=False, last=True, width=rem, chunks=edge_chunks, extra=extra)

    if has_extra:
        steps(i == 0, True)
        steps(i > 0, False)
    else:
        steps(True, False)


def _ffn(h, pre_g, wg, wu, wd, post_g, *, tm, tf, edge_chunks, extra=None):
    m, d = h.shape
    f = wg.shape[1]
    nj = pl.cdiv(f, tf)
    rem = f - (nj - 1) * tf
    assert nj >= 2 and m % tm == 0 and sum(edge_chunks) == tm
    assert rem % MXU_DIM == TAIL, "d_ff ends in half an MXU tile"
    w_tail = jnp.concatenate([wg[:, f - TAIL:], wu[:, f - TAIL:]], axis=1)
    has_extra = extra is not None
    ne = extra.shape[0] if has_extra else 0
    assert ne % BF16_ROWS == 0
    row = pl.BlockSpec((1, d), lambda i, j: (0, 0))
    e_spec = [pl.BlockSpec((ne, d), lambda i, j: (0, 0))] if has_extra else []
    out = pl.pallas_call(
        functools.partial(_ffn_kernel, rem=rem, edge_chunks=edge_chunks, has_extra=has_extra),
        grid=(m // tm, nj),
        in_specs=[pl.BlockSpec((tm, d), lambda i, j: (i, 0))] + e_spec + [
            row,
            pl.BlockSpec((d, tf), lambda i, j: (0, j)),
            pl.BlockSpec((d, tf), lambda i, j: (0, j)),
            pl.BlockSpec((tf, d), lambda i, j: (j, 0)),
            pl.BlockSpec((d, 2 * TAIL), lambda i, j: (0, 0), pipeline_mode=pl.Buffered(1)),
            row,
        ],
        out_specs=[pl.BlockSpec((tm, d), lambda i, j: (i, 0))] + e_spec,
        out_shape=[jax.ShapeDtypeStruct((m, d), F32)] + ([jax.ShapeDtypeStruct((ne, d), F32)] if has_extra else []),
        scratch_shapes=[pltpu.VMEM((tm + ne, d), BF16)],
        compiler_params=pltpu.CompilerParams(
            dimension_semantics=("arbitrary" if has_extra else "parallel", "arbitrary"),
            vmem_limit_bytes=VMEM_LIMIT),
        name="ffn",
    )(h, *([extra] if has_extra else []), pre_g, wg, wu, wd, w_tail, post_g)
    return out if has_extra else out[0]


def _shift_rows(x, k, head):
    r = pltpu.roll(x, k, 0)
    hr = pltpu.roll(head, k, 0)
    row = lax.broadcasted_iota(jnp.int32, head.shape, 0)
    first = jnp.where(row < k, hr, r[:SUBLANE])
    if x.shape[0] == SUBLANE:
        return first
    return jnp.concatenate([first, r[SUBLANE:]], axis=0)


def _shift_fill(x, s, fill):
    tt = x.shape[0]
    r = pltpu.roll(x, s, 0)
    row = lax.broadcasted_iota(jnp.int32, (SUBLANE,) + x.shape[1:], 0)
    first = jnp.where(row < s, fill, r[:SUBLANE])
    if tt == SUBLANE:
        return first
    return jnp.concatenate([first, r[SUBLANE:]], axis=0)


def _window_steps(a, u, steps):
    for s in steps:
        u = u + a * _shift_fill(u, s, 0.0)
        a = a * _shift_fill(a, s, 1.0)
    return a, u


def _scan_groups(a, u, h_in):
    hs = []
    h = h_in
    for g in range(a.shape[0] // SUBLANE):
        rows = slice(g * SUBLANE, (g + 1) * SUBLANE)
        h = a[rows] * h + u[rows]
        hs.append(h)
    return jnp.concatenate(hs, axis=0)


def _group_mean_sq(y, gmat):
    y2 = y * y
    hi = y2.astype(BF16)
    lo = (y2 - hi.astype(F32)).astype(BF16)
    return jnp.dot(jnp.concatenate([hi, lo], axis=1), gmat, preferred_element_type=F32)


def _gelu_tanh(x):
    return 0.5 * x * (1.0 + jnp.tanh(0.7978845608028654 * (x + 0.044715 * (x * x * x))))


def _mix_kernel(h_ref, pre_g_ref, win_ref, cw_ref, cb_ref, wax_ref, ba_ref, bx_ref, lam_ref, sw_ref, lg_ref,
                sg_ref, gmat_ref, wout_ref, post_g_ref, xh0_ref, cvh0_ref, h0_ref,
                o_ref, xt_ref, cvt_ref, hl_ref,
                xh_s, cvh_s, hc_s, un_s, z0_s, z1_s, m0_s, m1_s, *, d_lru):
    @pl.when(pl.program_id(1) == 0)
    def _():
        xh_s[...] = xh0_ref[...]
        cvh_s[...] = cvh0_ref[...]
        hc_s[...] = h0_ref[...]

    tt, d = h_ref.shape
    w = MXU_DIM
    nc = d_lru // w
    z_slots = (z0_s, z1_s)
    m_slots = (m0_s, m1_s)
    split_last = tt >= 2 * BF16_ROWS
    gmat = gmat_ref[...]
    un_s[...] = _rms(h_ref[...], pre_g_ref[...]).astype(BF16)

    def project_pieces(c):
        def piece(k):
            def run():
                col = k * d_lru + c * w
                z_slots[c % 2][:, k * w:(k + 1) * w] = jnp.dot(
                    un_s[...], win_ref[:, col:col + w], preferred_element_type=F32)
            return run
        return [piece(k) for k in range(N_Z)]

    def outproj_pieces(c):
        half = d // 2
        wrows = slice(c * 2 * w, (c + 1) * 2 * w)

        def piece(n):
            def run():
                ncols = slice(n * half, (n + 1) * half)
                part = jnp.dot(m_slots[c % 2][...], wout_ref[wrows, ncols], preferred_element_type=F32)
                if c == 0:
                    o_ref[:, ncols] = part
                else:
                    o_ref[:, ncols] += part
            return run

        def last_piece(r):
            def run():
                rows = pl.ds(r * (tt // 2), tt // 2)
                acc = jnp.dot(m_slots[c % 2][rows, :], wout_ref[wrows, :], preferred_element_type=F32)
                if c > 0:
                    acc = o_ref[rows, :] + acc
                o_ref[rows, :] = h_ref[rows, :] + _rms(acc, post_g_ref[...])
            return run

        if c == nc - 1 and split_last:
            return [last_piece(0), last_piece(1)]
        return [piece(0), piece(1)]

    def mixer(c):
        z = z_slots[c % 2]
        m = m_slots[c % 2]
        cols = slice(c * w, (c + 1) * w)
        x = z[:, w:2 * w]
        head = xh_s[:, cols]
        xc = (cb_ref[:, cols] + cw_ref[3:4, cols] * x
              + cw_ref[2:3, cols] * _shift_rows(x, 1, head)
              + cw_ref[1:2, cols] * _shift_rows(x, 2, head)
              + cw_ref[0:1, cols] * _shift_rows(x, 3, head))
        xh_s[:, cols] = x[tt - SUBLANE:]
        pre = jnp.dot(xc.astype(BF16), wax_ref[c], preferred_element_type=F32)
        yield
        cv = z[:, 3 * w:4 * w] * z[:, 4 * w:5 * w]
        chead = cvh_s[:, cols]
        sc = z[:, 2 * w:3 * w] * (sw_ref[2:3, cols] * cv
                                  + sw_ref[1:2, cols] * _shift_rows(cv, 1, chead)
                                  + sw_ref[0:1, cols] * _shift_rows(cv, 2, chead))
        cvh_s[:, cols] = cv[tt - SUBLANE:]
        sc_ms = _group_mean_sq(sc, gmat)
        yield
        ga = 0.5 + 0.5 * jnp.tanh(pre[:, :w] + 0.5 * ba_ref[:, cols])
        gx = 0.5 + 0.5 * jnp.tanh(pre[:, w:] + 0.5 * bx_ref[:, cols])
        nl = -lam_ref[:, cols]
        sp = jnp.maximum(nl, 0.0) + jnp.log1p(jnp.exp(-jnp.abs(nl)))
        log_a = ga * (-LRU_C * sp)
        a = jnp.exp(log_a)
        m2 = -jnp.tanh(log_a) * (a * a + 1.0)
        mult = jnp.where(m2 > 0.0, m2 * lax.rsqrt(m2), 0.0)
        u = mult * (gx * xc)
        yield
        a, u = _window_steps(a, u, (1, 2))
        yield
        a, u = _window_steps(a, u, (4,))
        h = _scan_groups(a, u, hc_s[:, cols])
        hc_s[:, cols] = h[tt - 1:]
        yield
        lo = h * _gelu_tanh(z[:, 0:w])
        lo_ms = _group_mean_sq(lo, gmat)
        yield
        m[:, w:] = (sc * lax.rsqrt(sc_ms + EPS) * sg_ref[:, cols]).astype(BF16)
        m[:, :w] = (lo * lax.rsqrt(lo_ms + EPS) * lg_ref[:, cols]).astype(BF16)
        yield

    for run in project_pieces(0):
        run()
    pending = []
    for c in range(nc):
        if c + 1 < nc:
            pending += project_pieces(c + 1)
        for _ in mixer(c):
            if pending:
                pending.pop(0)()
        while pending:
            pending.pop(0)()
        pending = outproj_pieces(c)
    for run in pending:
        run()

    if not split_last:
        o_ref[...] = h_ref[...] + _rms(o_ref[...], post_g_ref[...])
    xt_ref[...] = xh_s[...]
    cvt_ref[...] = cvh_s[...]
    hl_ref[0] = hc_s[...]


def _mix(h, p, xh0, cvh0, h0, *, nb, tt):
    rows, d = h.shape
    d_lru = p["cb"].shape[1]
    nt = rows // (nb * tt)
    assert rows == nb * nt * tt
    const = lambda b, t: (0, 0)
    once = pl.Buffered(1)
    vec = pl.BlockSpec((1, d_lru), const)
    gvec = pl.BlockSpec((1, d), const)
    state8 = jax.ShapeDtypeStruct((nb * SUBLANE, d_lru), F32)
    return pl.pallas_call(
        functools.partial(_mix_kernel, d_lru=d_lru),
        grid=(nb, nt),
        in_specs=[
            pl.BlockSpec((tt, d), lambda b, t: (b * nt + t, 0)),
            gvec,
            pl.BlockSpec(p["win"].shape, const, pipeline_mode=once),
            pl.BlockSpec(p["cw"].shape, const), vec,
            pl.BlockSpec(p["wax"].shape, lambda b, t: (0, 0, 0)),
            vec, vec, vec,
            pl.BlockSpec(p["sw"].shape, const), vec, vec,
            pl.BlockSpec((2 * MXU_DIM, MXU_DIM), const),
            pl.BlockSpec(p["wout"].shape, const, pipeline_mode=once),
            gvec,
            pl.BlockSpec((SUBLANE, d_lru), const),
            pl.BlockSpec((SUBLANE, d_lru), const),
            vec,
        ],
        out_specs=[
            pl.BlockSpec((tt, d), lambda b, t: (b * nt + t, 0)),
            pl.BlockSpec((SUBLANE, d_lru), lambda b, t: (b, 0)),
            pl.BlockSpec((SUBLANE, d_lru), lambda b, t: (b, 0)),
            pl.BlockSpec((1, 1, d_lru), lambda b, t: (b, 0, 0)),
        ],
        out_shape=[
            jax.ShapeDtypeStruct((rows, d), F32),
            state8, state8,
            jax.ShapeDtypeStruct((nb, 1, d_lru), F32),
        ],
        scratch_shapes=[
            pltpu.VMEM((SUBLANE, d_lru), F32),
            pltpu.VMEM((SUBLANE, d_lru), F32),
            pltpu.VMEM((1, d_lru), F32),
            pltpu.VMEM((tt, d), BF16),
            pltpu.VMEM((tt, N_Z * MXU_DIM), F32),
            pltpu.VMEM((tt, N_Z * MXU_DIM), F32),
            pltpu.VMEM((tt, 2 * MXU_DIM), BF16),
            pltpu.VMEM((tt, 2 * MXU_DIM), BF16),
        ],
        compiler_params=pltpu.CompilerParams(
            dimension_semantics=("parallel", "arbitrary"), vmem_limit_bytes=VMEM_LIMIT),
        name="mix",
    )(h, p["pre_g"], p["win"], p["cw"], p["cb"], p["wax"], p["ba"], p["bx"], p["lam"], p["sw"], p["lg"],
      p["sg"], p["gmat"], p["wout"], p["post_g"], xh0, cvh0, h0)


def _block_diag_gates(w_a, w_x):
    nh, blk, _ = w_a.shape
    per = MXU_DIM // blk
    eye = jnp.eye(per, dtype=w_a.dtype)

    def bd(w):
        w = w.reshape(nh // per, per, blk, blk)
        return jnp.einsum("cpij,pq->cpiqj", w, eye).reshape(nh // per, MXU_DIM, MXU_DIM)

    return jnp.concatenate([bd(w_a), bd(w_x)], axis=-1).astype(BF16)


def kernel(x, meta_tokens, ffn1_pre_g, ffn1_w_gate, ffn1_w_up, ffn1_w_down, ffn1_post_g, mix_pre_g, w_in,
           lru_conv_w, lru_conv_b, lru_w_a, lru_b_a, lru_w_x, lru_b_x, lru_lambda, sconv_w, lru_out_g,
           sconv_out_g, w_out, mix_post_g, ffn2_pre_g, ffn2_w_gate, ffn2_w_up, ffn2_w_down, ffn2_post_g):
    bn, seq, d = x.shape
    n_meta = meta_tokens.shape[0]
    d_lru = lru_conv_w.shape[-1]
    d_sc = sconv_w.shape[-1]
    assert ffn1_pre_g.shape[0] == 1, "single layer"
    assert n_meta % BF16_ROWS == 0
    assert d_lru == d_sc == N_LRU_HEADS * HEAD_DIM == N_SCONV_GROUPS * HEAD_DIM
    assert lru_w_a.shape[-1] == HEAD_DIM and d_lru % MXU_DIM == 0
    assert w_in.shape[-1] == N_Z * d_lru and w_out.shape[1] == 2 * d_lru

    w1 = tuple(w[0].astype(BF16) for w in (ffn1_w_gate, ffn1_w_up, ffn1_w_down))
    w2 = tuple(w[0].astype(BF16) for w in (ffn2_w_gate, ffn2_w_up, ffn2_w_down))
    head = jnp.arange(MXU_DIM) // HEAD_DIM
    mp = {
        "pre_g": mix_pre_g,
        "win": w_in[0].astype(BF16),
        "cw": lru_conv_w[0], "cb": lru_conv_b,
        "wax": _block_diag_gates(0.5 * lru_w_a[0], 0.5 * lru_w_x[0]),
        "ba": lru_b_a, "bx": lru_b_x, "lam": lru_lambda,
        "sw": sconv_w[0], "lg": lru_out_g, "sg": sconv_out_g,
        "gmat": jnp.tile(((head[:, None] == head[None, :]).astype(F32) / HEAD_DIM).astype(BF16), (2, 1)),
        "wout": w_out[0].reshape(2, d_lru // MXU_DIM, MXU_DIM, d).transpose(1, 0, 2, 3)
                        .reshape(2 * d_lru, d).astype(BF16),
        "post_g": mix_post_g,
    }
    ffn_tiles = dict(tm=1024, tf=512, edge_chunks=(512, 512))

    h = x.reshape(bn * seq, d)
    h, hm = _ffn(h, ffn1_pre_g, *w1, ffn1_post_g, extra=meta_tokens, **ffn_tiles)
    zero8 = jnp.zeros((SUBLANE, d_lru), F32)
    _, xt, cvt, hl = _mix(hm, mp, zero8, zero8, jnp.zeros((1, d_lru), F32), nb=1, tt=n_meta)
    h, _, _, _ = _mix(h, mp, xt, cvt, hl[0], nb=bn, tt=512)
    h = _ffn(h, ffn2_pre_g, *w2, ffn2_post_g, **ffn_tiles)
    return h.reshape(bn, seq, d)
```

```python
import functools

import jax
import jax.numpy as jnp
from jax import lax
from jax.experimental import pallas as pl
from jax.experimental.pallas import tpu as pltpu

EPS = 1e-6
N_LRU_HEADS = 16
N_SCONV_GROUPS = 16
HEAD_DIM = 64
LRU_C = 8.0
SUBLANE = 8
BF16_ROWS = 2 * SUBLANE
MXU_DIM = 256
TAIL = MXU_DIM // 2
VMEM_LIMIT = 56 * 1024 * 1024
N_Z = 5

F32 = jnp.float32
BF16 = jnp.bfloat16


def _sigmoid(x):
    return 0.5 + 0.5 * jnp.tanh(0.5 * x)


def _rms(x, g):
    ms = jnp.mean(x * x, axis=-1, keepdims=True)
    return x * lax.rsqrt(ms + EPS) * g


def _silu_mul(g, u):
    return (g * _sigmoid(g) * u).astype(BF16)


def _ffn_step(h_ref, e_ref, pre_g_ref, wg_ref, wu_ref, wd_ref, wt_ref, post_g_ref, o_ref, eo_ref, xn_ref, *,
              first, last, width, chunks, extra):
    tm = h_ref.shape[0]
    r0 = 0
    for ci, rc in enumerate(chunks):
        rows = pl.ds(r0, rc)
        ride = e_ref.shape[0] if extra and ci == len(chunks) - 1 else 0
        if first:
            xn_ref[rows, :] = _rms(h_ref[rows, :], pre_g_ref[...]).astype(BF16)
            if ride:
                xn_ref[pl.ds(tm, ride), :] = _rms(e_ref[...], pre_g_ref[...]).astype(BF16)
        nr = rc + ride
        xn = xn_ref[pl.ds(r0, nr), :]
        wm = width - TAIL if last else width
        g = jnp.dot(xn, wg_ref[:, :wm], preferred_element_type=F32)
        u = jnp.dot(xn, wu_ref[:, :wm], preferred_element_type=F32)
        acc = jnp.dot(_silu_mul(g, u), wd_ref[:wm, :], preferred_element_type=F32)
        if last:
            cuts = (0, rc // 2, nr) if rc >= 2 * BF16_ROWS else (0, nr)
            gu = jnp.concatenate([jnp.dot(xn[lo:hi], wt_ref[...], preferred_element_type=F32)
                                  for lo, hi in zip(cuts[:-1], cuts[1:])], axis=0)
            acc += jnp.dot(_silu_mul(gu[:, :TAIL], gu[:, TAIL:]), wd_ref[wm:width, :],
                           preferred_element_type=F32)
        main = acc[:rc]
        if not first:
            main = o_ref[rows, :] + main
        o_ref[rows, :] = main
        if last:
            o_ref[rows, :] = h_ref[rows, :] + _rms(o_ref[rows, :], 0.5 * post_g_ref[...])
        if ride:
            ext = acc[rc:]
            if not first:
                ext = eo_ref[...] + ext
            if last:
                ext = e_ref[...] + _rms(ext, 0.5 * post_g_ref[...])
            eo_ref[...] = ext
        r0 += rc


def _ffn_kernel(*refs, rem, edge_chunks, has_extra):
    if has_extra:
        h_ref, e_ref, pre_g_ref, wg_ref, wu_ref, wd_ref, wt_ref, post_g_ref, o_ref, eo_ref, xn_ref = refs
    else:
        h_ref, pre_g_ref, wg_ref, wu_ref, wd_ref, wt_ref, post_g_ref, o_ref, xn_ref = refs
        e_ref = eo_ref = None
    i = pl.program_id(0)
    j = pl.program_id(1)
    nj = pl.num_programs(1)
    tm = h_ref.shape[0]
    tf = wg_ref.shape[1]

    def step(**kw):
        _ffn_step(h_ref, e_ref, pre_g_ref, wg_ref, wu_ref, wd_ref, wt_ref, post_g_ref, o_ref, eo_ref, xn_ref, **kw)

    def steps(tile_cond, extra):
        @pl.when(jnp.logical_and(tile_cond, j == 0))
        def _():
            step(first=True, last=False, width=tf, chunks=edge_chunks, extra=extra)

        @pl.when(jnp.logical_and(tile_cond, jnp.logical_and(j > 0, j < nj - 1)))
        def _():
            step(first=False, last=False, width=tf, chunks=(tm,), extra=extra)

        @pl.when(jnp.logical_and(tile_cond, j == nj - 1))
        def _():
            step(first=False, last=True, width=rem, chunks=edge_chunks, extra=extra)

    if has_extra:
        steps(i == 0, True)
        steps(i > 0, False)

        @pl.when(jnp.logical_and(i > 0, j == nj - 1))
        def _():
            eo_ref[...] = jnp.zeros_like(eo_ref)
    else:
        steps(True, False)


def _ffn(h, pre_g, wg, wu, wd, post_g, *, tm, tf, edge_chunks, extra=None):
    m, d = h.shape
    f = wg.shape[1]
    nj = pl.cdiv(f, tf)
    rem = f - (nj - 1) * tf
    assert nj >= 2 and m % tm == 0 and sum(edge_chunks) == tm
    assert rem % MXU_DIM == TAIL, "d_ff ends in half an MXU tile"
    w_tail = jnp.concatenate([wg[:, f - TAIL:], wu[:, f - TAIL:]], axis=1)
    has_extra = extra is not None
    ne = extra.shape[0] if has_extra else 0
    assert ne % BF16_ROWS == 0
    row = pl.BlockSpec((1, d), lambda i, j: (0, 0))
    e_spec = [pl.BlockSpec((ne, d), lambda i, j: (0, 0))] if has_extra else []
    out = pl.pallas_call(
        functools.partial(_ffn_kernel, rem=rem, edge_chunks=edge_chunks, has_extra=has_extra),
        grid=(m // tm, nj),
        in_specs=[pl.BlockSpec((tm, d), lambda i, j: (i, 0))] + e_spec + [
            row,
            pl.BlockSpec((d, tf), lambda i, j: (0, j)),
            pl.BlockSpec((d, tf), lambda i, j: (0, j)),
            pl.BlockSpec((tf, d), lambda i, j: (j, 0)),
            pl.BlockSpec((d, 2 * TAIL), lambda i, j: (0, 0), pipeline_mode=pl.Buffered(1)),
            row,
        ],
        out_specs=[pl.BlockSpec((tm, d), lambda i, j: (i, 0))]
        + ([pl.BlockSpec((ne, d), lambda i, j: (i, 0))] if has_extra else []),
        out_shape=[jax.ShapeDtypeStruct((m, d), F32)]
        + ([jax.ShapeDtypeStruct((m // tm * ne, d), F32)] if has_extra else []),
        scratch_shapes=[pltpu.VMEM((tm + ne, d), BF16)],
        compiler_params=pltpu.CompilerParams(
            dimension_semantics=("parallel", "arbitrary"), vmem_limit_bytes=VMEM_LIMIT),
        name="ffn",
    )(h, *([extra] if has_extra else []), pre_g, wg, wu, wd, w_tail, post_g)
    return (out[0], out[1][:ne]) if has_extra else out[0]


def _shift_rows(x, k, head):
    r = pltpu.roll(x, k, 0)
    hr = pltpu.roll(head, k, 0)
    row = lax.broadcasted_iota(jnp.int32, head.shape, 0)
    first = jnp.where(row < k, hr, r[:SUBLANE])
    if x.shape[0] == SUBLANE:
        return first
    return jnp.concatenate([first, r[SUBLANE:]], axis=0)


def _shift_fill(x, s, fill):
    tt = x.shape[0]
    r = pltpu.roll(x, s, 0)
    row = lax.broadcasted_iota(jnp.int32, (SUBLANE,) + x.shape[1:], 0)
    first = jnp.where(row < s, fill, r[:SUBLANE])
    if tt == SUBLANE:
        return first
    return jnp.concatenate([first, r[SUBLANE:]], axis=0)


def _window_steps(a, u, steps):
    for s in steps:
        u = u + a * _shift_fill(u, s, 0.0)
        a = a * _shift_fill(a, s, 1.0)
    return a, u


def _scan_groups(a, u, h_in):
    hs = []
    h = h_in
    for g in range(a.shape[0] // SUBLANE):
        rows = slice(g * SUBLANE, (g + 1) * SUBLANE)
        h = a[rows] * h + u[rows]
        hs.append(h)
    return jnp.concatenate(hs, axis=0)


def _group_mean_sq(y, gmat):
    y2 = y * y
    hi = y2.astype(BF16)
    lo = (y2 - hi.astype(F32)).astype(BF16)
    return jnp.dot(jnp.concatenate([hi, lo], axis=1), gmat, preferred_element_type=F32)


def _gelu_tanh(x):
    return 0.5 * x * (1.0 + jnp.tanh(0.7978845608028654 * (x + 0.044715 * (x * x * x))))


def _mix_kernel(h_ref, pre_g_ref, win_ref, cw_ref, cb_ref, wax_ref, ba_ref, bx_ref, lam_ref, sw_ref, lg_ref,
                sg_ref, gmat_ref, wout_ref, post_g_ref, xh0_ref, cvh0_ref, h0_ref,
                o_ref, xt_ref, cvt_ref, hl_ref,
                xh_s, cvh_s, hc_s, un_s, z0_s, z1_s, m0_s, m1_s, *, d_lru):
    @pl.when(pl.program_id(1) == 0)
    def _():
        xh_s[...] = xh0_ref[...]
        cvh_s[...] = cvh0_ref[...]
        hc_s[...] = h0_ref[...]

    tt, d = h_ref.shape
    w = MXU_DIM
    nc = d_lru // w
    z_slots = (z0_s, z1_s)
    m_slots = (m0_s, m1_s)
    split_last = tt >= 2 * BF16_ROWS
    gmat = gmat_ref[...]
    un_s[...] = _rms(h_ref[...], pre_g_ref[...]).astype(BF16)

    def project_pieces(c):
        def piece(k):
            def run():
                col = k * d_lru + c * w
                z_slots[c % 2][:, k * w:(k + 1) * w] = jnp.dot(
                    un_s[...], win_ref[:, col:col + w], preferred_element_type=F32)
            return run
        return [piece(k) for k in range(N_Z)]

    def outproj_pieces(c):
        half = d // 2
        wrows = slice(c * 2 * w, (c + 1) * 2 * w)

        def piece(n):
            def run():
                ncols = slice(n * half, (n + 1) * half)
                part = jnp.dot(m_slots[c % 2][...], wout_ref[wrows, ncols], preferred_element_type=F32)
                if c == 0:
                    o_ref[:, ncols] = part
                else:
                    o_ref[:, ncols] += part
            return run

        def last_piece(r):
            def run():
                rows = pl.ds(r * (tt // 2), tt // 2)
                acc = jnp.dot(m_slots[c % 2][rows, :], wout_ref[wrows, :], preferred_element_type=F32)
                if c > 0:
                    acc = o_ref[rows, :] + acc
                o_ref[rows, :] = h_ref[rows, :] + _rms(acc, post_g_ref[...])
            return run

        if c == nc - 1 and split_last:
            return [last_piece(0), last_piece(1)]
        return [piece(0), piece(1)]

    def mixer(c):
        z = z_slots[c % 2]
        m = m_slots[c % 2]
        cols = slice(c * w, (c + 1) * w)
        x = z[:, w:2 * w]
        head = xh_s[:, cols]
        xc = (cb_ref[:, cols] + cw_ref[3:4, cols] * x
              + cw_ref[2:3, cols] * _shift_rows(x, 1, head)
              + cw_ref[1:2, cols] * _shift_rows(x, 2, head)
              + cw_ref[0:1, cols] * _shift_rows(x, 3, head))
        xh_s[:, cols] = x[tt - SUBLANE:]
        pre = jnp.dot(xc.astype(BF16), wax_ref[c], preferred_element_type=F32)
        yield
        cv = z[:, 3 * w:4 * w] * z[:, 4 * w:5 * w]
        chead = cvh_s[:, cols]
        sc = z[:, 2 * w:3 * w] * (sw_ref[2:3, cols] * cv
                                  + sw_ref[1:2, cols] * _shift_rows(cv, 1, chead)
                                  + sw_ref[0:1, cols] * _shift_rows(cv, 2, chead))
        cvh_s[:, cols] = cv[tt - SUBLANE:]
        sc_ms = _group_mean_sq(sc, gmat)
        yield
        ga = 0.5 + 0.5 * jnp.tanh(pre[:, :w] + 0.5 * ba_ref[:, cols])
        gx = 0.5 + 0.5 * jnp.tanh(pre[:, w:] + 0.5 * bx_ref[:, cols])
        nl = -lam_ref[:, cols]
        sp = jnp.maximum(nl, 0.0) + jnp.log1p(jnp.exp(-jnp.abs(nl)))
        log_a = ga * (-LRU_C * sp)
        a = jnp.exp(log_a)
        m2 = -jnp.tanh(log_a) * (a * a + 1.0)
        mult = jnp.where(m2 > 0.0, m2 * lax.rsqrt(m2), 0.0)
        u = mult * (gx * xc)
        yield
        a, u = _window_steps(a, u, (1, 2))
        yield
        a, u = _window_steps(a, u, (4,))
        h = _scan_groups(a, u, hc_s[:, cols])
        hc_s[:, cols] = h[tt - 1:]
        yield
        lo = h * _gelu_tanh(z[:, 0:w])
        lo_ms = _group_mean_sq(lo, gmat)
        yield
        m[:, w:] = (sc * lax.rsqrt(sc_ms + EPS) * sg_ref[:, cols]).astype(BF16)
        m[:, :w] = (lo * lax.rsqrt(lo_ms + EPS) * lg_ref[:, cols]).astype(BF16)
        yield

    for run in project_pieces(0):
        run()
    pending = []
    for c in range(nc):
        if c + 1 < nc:
            pending += project_pieces(c + 1)
        for _ in mixer(c):
            if pending:
                pending.pop(0)()
        while pending:
            pending.pop(0)()
        pending = outproj_pieces(c)
    for run in pending:
        run()

    if not split_last:
        o_ref[...] = h_ref[...] + _rms(o_ref[...], post_g_ref[...])
    xt_ref[...] = xh_s[...]
    cvt_ref[...] = cvh_s[...]
    hl_ref[0] = hc_s[...]


def _mix(h, p, xh0, cvh0, h0, *, nb, tt):
    rows, d = h.shape
    d_lru = p["cb"].shape[1]
    nt = rows // (nb * tt)
    assert rows == nb * nt * tt
    const = lambda b, t: (0, 0)
    once = pl.Buffered(1)
    vec = pl.BlockSpec((1, d_lru), const)
    gvec = pl.BlockSpec((1, d), const)
    state8 = jax.ShapeDtypeStruct((nb * SUBLANE, d_lru), F32)
    return pl.pallas_call(
        functools.partial(_mix_kernel, d_lru=d_lru),
        grid=(nb, nt),
        in_specs=[
            pl.BlockSpec((tt, d), lambda b, t: (b * nt + t, 0)),
            gvec,
            pl.BlockSpec(p["win"].shape, const, pipeline_mode=once),
            pl.BlockSpec(p["cw"].shape, const), vec,
            pl.BlockSpec(p["wax"].shape, lambda b, t: (0, 0, 0)),
            vec, vec, vec,
            pl.BlockSpec(p["sw"].shape, const), vec, vec,
            pl.BlockSpec((2 * MXU_DIM, MXU_DIM), const),
            pl.BlockSpec(p["wout"].shape, const, pipeline_mode=once),
            gvec,
            pl.BlockSpec((SUBLANE, d_lru), const),
            pl.BlockSpec((SUBLANE, d_lru), const),
            vec,
        ],
        out_specs=[
            pl.BlockSpec((tt, d), lambda b, t: (b * nt + t, 0)),
            pl.BlockSpec((SUBLANE, d_lru), lambda b, t: (b, 0)),
            pl.BlockSpec((SUBLANE, d_lru), lambda b, t: (b, 0)),
            pl.BlockSpec((1, 1, d_lru), lambda b, t: (b, 0, 0)),
        ],
        out_shape=[
            jax.ShapeDtypeStruct((rows, d), F32),
            state8, state8,
            jax.ShapeDtypeStruct((nb, 1, d_lru), F32),
        ],
        scratch_shapes=[
            pltpu.VMEM((SUBLANE, d_lru), F32),
            pltpu.VMEM((SUBLANE, d_lru), F32),
            pltpu.VMEM((1, d_lru), F32),
            pltpu.VMEM((tt, d), BF16),
            pltpu.VMEM((tt, N_Z * MXU_DIM), F32),
            pltpu.VMEM((tt, N_Z * MXU_DIM), F32),
            pltpu.VMEM((tt, 2 * MXU_DIM), BF16),
            pltpu.VMEM((tt, 2 * MXU_DIM), BF16),
        ],
        compiler_params=pltpu.CompilerParams(
            dimension_semantics=("parallel", "arbitrary"), vmem_limit_bytes=VMEM_LIMIT),
        name="mix",
    )(h, p["pre_g"], p["win"], p["cw"], p["cb"], p["wax"], p["ba"], p["bx"], p["lam"], p["sw"], p["lg"],
      p["sg"], p["gmat"], p["wout"], p["post_g"], xh0, cvh0, h0)


def _block_diag_gates(w_a, w_x):
    nh, blk, _ = w_a.shape
    per = MXU_DIM // blk
    eye = jnp.eye(per, dtype=w_a.dtype)

    def bd(w):
        w = w.reshape(nh // per, per, blk, blk)
        return jnp.einsum("cpij,pq->cpiqj", w, eye).reshape(nh // per, MXU_DIM, MXU_DIM)

    return jnp.concatenate([bd(w_a), bd(w_x)], axis=-1).astype(BF16)


def kernel(x, meta_tokens, ffn1_pre_g, ffn1_w_gate, ffn1_w_up, ffn1_w_down, ffn1_post_g, mix_pre_g, w_in,
           lru_conv_w, lru_conv_b, lru_w_a, lru_b_a, lru_w_x, lru_b_x, lru_lambda, sconv_w, lru_out_g,
           sconv_out_g, w_out, mix_post_g, ffn2_pre_g, ffn2_w_gate, ffn2_w_up, ffn2_w_down, ffn2_post_g):
    bn, seq, d = x.shape
    n_meta = meta_tokens.shape[0]
    d_lru = lru_conv_w.shape[-1]
    d_sc = sconv_w.shape[-1]
    assert ffn1_pre_g.shape[0] == 1, "single layer"
    assert n_meta % BF16_ROWS == 0
    assert d_lru == d_sc == N_LRU_HEADS * HEAD_DIM == N_SCONV_GROUPS * HEAD_DIM
    assert lru_w_a.shape[-1] == HEAD_DIM and d_lru % MXU_DIM == 0
    assert w_in.shape[-1] == N_Z * d_lru and w_out.shape[1] == 2 * d_lru

    w1 = tuple(w[0].astype(BF16) for w in (ffn1_w_gate, ffn1_w_up, ffn1_w_down))
    w2 = tuple(w[0].astype(BF16) for w in (ffn2_w_gate, ffn2_w_up, ffn2_w_down))
    head = jnp.arange(MXU_DIM) // HEAD_DIM
    mp = {
        "pre_g": mix_pre_g,
        "win": w_in[0].astype(BF16),
        "cw": lru_conv_w[0], "cb": lru_conv_b,
        "wax": _block_diag_gates(0.5 * lru_w_a[0], 0.5 * lru_w_x[0]),
        "ba": lru_b_a, "bx": lru_b_x, "lam": lru_lambda,
        "sw": sconv_w[0], "lg": lru_out_g, "sg": sconv_out_g,
        "gmat": jnp.tile(((head[:, None] == head[None, :]).astype(F32) / HEAD_DIM).astype(BF16), (2, 1)),
        "wout": w_out[0].reshape(2, d_lru // MXU_DIM, MXU_DIM, d).transpose(1, 0, 2, 3)
                        .reshape(2 * d_lru, d).astype(BF16),
        "post_g": mix_post_g,
    }
    ffn_tiles = dict(tm=1024, tf=512, edge_chunks=(512, 512))

    h = x.reshape(bn * seq, d)
    h, hm = _ffn(h, ffn1_pre_g, *w1, ffn1_post_g, extra=meta_tokens, **ffn_tiles)
    zero8 = jnp.zeros((SUBLANE, d_lru), F32)
    _, xt, cvt, hl = _mix(hm, mp, zero8, zero8, jnp.zeros((1, d_lru), F32), nb=1, tt=n_meta)
    h, _, _, _ = _mix(h, mp, xt, cvt, hl[0], nb=bn, tt=512)
    h = _ffn(h, ffn2_pre_g, *w2, ffn2_post_g, **ffn_tiles)
    return h.reshape(bn, seq, d)
```

```python
import functools

import jax
import jax.numpy as jnp
from jax import lax
from jax.experimental import pallas as pl
from jax.experimental.pallas import tpu as pltpu

EPS = 1e-6
N_LRU_HEADS = 16
N_SCONV_GROUPS = 16
HEAD_DIM = 64
LRU_C = 8.0
SUBLANE = 8
BF16_ROWS = 2 * SUBLANE
MXU_DIM = 256
TAIL = MXU_DIM // 2
VMEM_LIMIT = 56 * 1024 * 1024
N_Z = 5

F32 = jnp.float32
BF16 = jnp.bfloat16


def _sigmoid(x):
    return 0.5 + 0.5 * jnp.tanh(0.5 * x)


def _rms(x, g):
    ms = jnp.mean(x * x, axis=-1, keepdims=True)
    return x * lax.rsqrt(ms + EPS) * g


def _silu_mul(g, u):
    return (g * _sigmoid(g) * u).astype(BF16)


def _ffn_step(h_ref, e_ref, pre_g_ref, wg_ref, wu_ref, wd_ref, wt_ref, post_g_ref, o_ref, eo_ref, xn_ref, *,
              first, last, width, chunks, extra):
    tm = h_ref.shape[0]
    r0 = 0
    for ci, rc in enumerate(chunks):
        rows = pl.ds(r0, rc)
        ride = e_ref.shape[0] if extra and ci == len(chunks) - 1 else 0
        if first:
            xn_ref[rows, :] = _rms(h_ref[rows, :], pre_g_ref[...]).astype(BF16)
            if ride:
                xn_ref[pl.ds(tm, ride), :] = _rms(e_ref[...], pre_g_ref[...]).astype(BF16)
        nr = rc + ride
        xn = xn_ref[pl.ds(r0, nr), :]
        wm = width - TAIL if last else width
        g = jnp.dot(xn, wg_ref[:, :wm], preferred_element_type=F32)
        u = jnp.dot(xn, wu_ref[:, :wm], preferred_element_type=F32)
        acc = jnp.dot(_silu_mul(g, u), wd_ref[:wm, :], preferred_element_type=F32)
        if last:
            cuts = (0, rc // 2, nr) if rc >= 2 * BF16_ROWS else (0, nr)
            gu = jnp.concatenate([jnp.dot(xn[lo:hi], wt_ref[...], preferred_element_type=F32)
                                  for lo, hi in zip(cuts[:-1], cuts[1:])], axis=0)
            acc += jnp.dot(_silu_mul(gu[:, :TAIL], gu[:, TAIL:]), wd_ref[wm:width, :],
                           preferred_element_type=F32)
        main = acc[:rc]
        if not first:
            main = o_ref[rows, :] + main
        o_ref[rows, :] = main
        if last:
            o_ref[rows, :] = h_ref[rows, :] + _rms(o_ref[rows, :], 0.5 * post_g_ref[...])
        if ride:
            ext = acc[rc:]
            if not first:
                ext = eo_ref[...] + ext
            if last:
                ext = e_ref[...] + _rms(ext, 0.5 * post_g_ref[...])
            eo_ref[...] = ext
        r0 += rc


def _ffn_kernel(*refs, rem, edge_chunks, has_extra):
    if has_extra:
        h_ref, e_ref, pre_g_ref, wg_ref, wu_ref, wd_ref, wt_ref, post_g_ref, o_ref, eo_ref, xn_ref = refs
    else:
        h_ref, pre_g_ref, wg_ref, wu_ref, wd_ref, wt_ref, post_g_ref, o_ref, xn_ref = refs
        e_ref = eo_ref = None
    i = pl.program_id(0)
    j = pl.program_id(1)
    nj = pl.num_programs(1)
    tm = h_ref.shape[0]
    tf = wg_ref.shape[1]

    def step(**kw):
        _ffn_step(h_ref, e_ref, pre_g_ref, wg_ref, wu_ref, wd_ref, wt_ref, post_g_ref, o_ref, eo_ref, xn_ref, **kw)

    def steps(tile_cond, extra):
        @pl.when(jnp.logical_and(tile_cond, j == 0))
        def _():
            step(first=True, last=False, width=tf, chunks=edge_chunks, extra=extra)

        @pl.when(jnp.logical_and(tile_cond, jnp.logical_and(j > 0, j < nj - 1)))
        def _():
            step(first=False, last=False, width=tf, chunks=(tm,), extra=extra)

        @pl.when(jnp.logical_and(tile_cond, j == nj - 1))
        def _():
            step(first=False, last=True, width=rem, chunks=edge_chunks, extra=extra)

    if has_extra:
        steps(i == 0, True)
        steps(i > 0, False)
    else:
        steps(True, False)


def _ffn(h, pre_g, wg, wu, wd, post_g, *, tm, tf, edge_chunks, extra=None):
    m, d = h.shape
    f = wg.shape[1]
    nj = pl.cdiv(f, tf)
    rem = f - (nj - 1) * tf
    assert nj >= 2 and m % tm == 0 and sum(edge_chunks) == tm
    assert rem % MXU_DIM == TAIL, "d_ff ends in half an MXU tile"
    w_tail = jnp.concatenate([wg[:, f - TAIL:], wu[:, f - TAIL:]], axis=1)
    has_extra = extra is not None
    ne = extra.shape[0] if has_extra else 0
    assert ne % BF16_ROWS == 0
    row = pl.BlockSpec((1, d), lambda i, j: (0, 0))
    e_spec = [pl.BlockSpec((ne, d), lambda i, j: (0, 0))] if has_extra else []
    out = pl.pallas_call(
        functools.partial(_ffn_kernel, rem=rem, edge_chunks=edge_chunks, has_extra=has_extra),
        grid=(m // tm, nj),
        in_specs=[pl.BlockSpec((tm, d), lambda i, j: (i, 0))] + e_spec + [
            row,
            pl.BlockSpec((d, tf), lambda i, j: (0, j)),
            pl.BlockSpec((d, tf), lambda i, j: (0, j)),
            pl.BlockSpec((tf, d), lambda i, j: (j, 0)),
            pl.BlockSpec((d, 2 * TAIL), lambda i, j: (0, 0), pipeline_mode=pl.Buffered(1)),
            row,
        ],
        out_specs=[pl.BlockSpec((tm, d), lambda i, j: (i, 0))] + e_spec,
        out_shape=[jax.ShapeDtypeStruct((m, d), F32)] + ([jax.ShapeDtypeStruct((ne, d), F32)] if has_extra else []),
        scratch_shapes=[pltpu.VMEM((tm + ne, d), BF16)],
        compiler_params=pltpu.CompilerParams(
            dimension_semantics=("arbitrary" if has_extra else "parallel", "arbitrary"),
            vmem_limit_bytes=VMEM_LIMIT),
        name="ffn",
    )(h, *([extra] if has_extra else []), pre_g, wg, wu, wd, w_tail, post_g)
    return out if has_extra else out[0]


def _shift_rows(x, k, head):
    r = pltpu.roll(x, k, 0)
    hr = pltpu.roll(head, k, 0)
    row = lax.broadcasted_iota(jnp.int32, head.shape, 0)
    first = jnp.where(row < k, hr, r[:SUBLANE])
    if x.shape[0] == SUBLANE:
        return first
    return jnp.concatenate([first, r[SUBLANE:]], axis=0)


def _shift_fill(x, s, fill):
    tt = x.shape[0]
    r = pltpu.roll(x, s, 0)
    row = lax.broadcasted_iota(jnp.int32, (SUBLANE,) + x.shape[1:], 0)
    first = jnp.where(row < s, fill, r[:SUBLANE])
    if tt == SUBLANE:
        return first
    return jnp.concatenate([first, r[SUBLANE:]], axis=0)


def _window_steps(a, u, steps):
    for s in steps:
        u = u + a * _shift_fill(u, s, 0.0)
        a = a * _shift_fill(a, s, 1.0)
    return a, u


def _scan_groups(a, u, h_in):
    hs = []
    h = h_in
    for g in range(a.shape[0] // SUBLANE):
        rows = slice(g * SUBLANE, (g + 1) * SUBLANE)
        h = a[rows] * h + u[rows]
        hs.append(h)
    return jnp.concatenate(hs, axis=0)


def _group_mean_sq(y, gmat):
    y2 = y * y
    hi = y2.astype(BF16)
    lo = (y2 - hi.astype(F32)).astype(BF16)
    return jnp.dot(jnp.concatenate([hi, lo], axis=1), gmat, preferred_element_type=F32)


def _gelu_tanh(x):
    c = 0.7978845608028654
    hx = 0.5 * x
    return hx + hx * jnp.tanh(x * (c + (c * 0.044715) * (x * x)))


def _mix_kernel(h_ref, pre_g_ref, win_ref, cw_ref, cb_ref, wax_ref, ba_ref, bx_ref, lam_ref, sw_ref, lg_ref,
                sg_ref, gmat_ref, wout_ref, post_g_ref, xh0_ref, cvh0_ref, h0_ref,
                o_ref, xt_ref, cvt_ref, hl_ref,
                xh_s, cvh_s, hc_s, un_s, z0_s, z1_s, m0_s, m1_s, *, d_lru):
    @pl.when(pl.program_id(1) == 0)
    def _():
        xh_s[...] = xh0_ref[...]
        cvh_s[...] = cvh0_ref[...]
        hc_s[...] = h0_ref[...]

    tt, d = h_ref.shape
    w = MXU_DIM
    nc = d_lru // w
    z_slots = (z0_s, z1_s)
    m_slots = (m0_s, m1_s)
    split_last = tt >= 2 * BF16_ROWS
    gmat = gmat_ref[...]
    un_s[...] = _rms(h_ref[...], pre_g_ref[...]).astype(BF16)

    def project_pieces(c):
        def piece(k):
            def run():
                col = k * d_lru + c * w
                z_slots[c % 2][:, k * w:(k + 1) * w] = jnp.dot(
                    un_s[...], win_ref[:, col:col + w], preferred_element_type=F32)
            return run
        return [piece(k) for k in range(N_Z)]

    def outproj_pieces(c):
        half = d // 2
        wrows = slice(c * 2 * w, (c + 1) * 2 * w)

        def piece(n):
            def run():
                ncols = slice(n * half, (n + 1) * half)
                part = jnp.dot(m_slots[c % 2][...], wout_ref[wrows, ncols], preferred_element_type=F32)
                if c == 0:
                    o_ref[:, ncols] = part
                else:
                    o_ref[:, ncols] += part
            return run

        def last_piece(r):
            def run():
                rows = pl.ds(r * (tt // 2), tt // 2)
                acc = jnp.dot(m_slots[c % 2][rows, :], wout_ref[wrows, :], preferred_element_type=F32)
                if c > 0:
                    acc = o_ref[rows, :] + acc
                o_ref[rows, :] = h_ref[rows, :] + _rms(acc, post_g_ref[...])
            return run

        if c == nc - 1 and split_last:
            return [last_piece(0), last_piece(1)]
        return [piece(0), piece(1)]

    def mixer(c):
        z = z_slots[c % 2]
        m = m_slots[c % 2]
        cols = slice(c * w, (c + 1) * w)
        x = z[:, w:2 * w]
        head = xh_s[:, cols]
        xc = (cb_ref[:, cols] + cw_ref[3:4, cols] * x
              + cw_ref[2:3, cols] * _shift_rows(x, 1, head)
              + cw_ref[1:2, cols] * _shift_rows(x, 2, head)
              + cw_ref[0:1, cols] * _shift_rows(x, 3, head))
        xh_s[:, cols] = x[tt - SUBLANE:]
        pre = jnp.dot(xc.astype(BF16), wax_ref[c], preferred_element_type=F32)
        yield
        cv = z[:, 3 * w:4 * w] * z[:, 4 * w:5 * w]
        chead = cvh_s[:, cols]
        sc = z[:, 2 * w:3 * w] * (sw_ref[2:3, cols] * cv
                                  + sw_ref[1:2, cols] * _shift_rows(cv, 1, chead)
                                  + sw_ref[0:1, cols] * _shift_rows(cv, 2, chead))
        cvh_s[:, cols] = cv[tt - SUBLANE:]
        sc_ms = _group_mean_sq(sc, gmat)
        yield
        ta = jnp.tanh(pre[:, :w] + 0.5 * ba_ref[:, cols])
        gx = 0.5 + 0.5 * jnp.tanh(pre[:, w:] + 0.5 * bx_ref[:, cols])
        nl = -lam_ref[:, cols]
        sp = jnp.maximum(nl, 0.0) + jnp.log1p(jnp.exp(-jnp.abs(nl)))
        hc = (-0.5 * LRU_C) * sp
        log_a = hc + hc * ta
        a = jnp.exp(log_a)
        m2 = -jnp.tanh(log_a) * (a * a + 1.0)
        mult = jnp.where(m2 > 0.0, m2 * lax.rsqrt(m2), 0.0)
        u = mult * (gx * xc)
        yield
        a, u = _window_steps(a, u, (1, 2))
        yield
        a, u = _window_steps(a, u, (4,))
        h = _scan_groups(a, u, hc_s[:, cols])
        hc_s[:, cols] = h[tt - 1:]
        yield
        lo = h * _gelu_tanh(z[:, 0:w])
        lo_ms = _group_mean_sq(lo, gmat)
        yield
        m[:, w:] = (sc * lax.rsqrt(sc_ms + EPS) * sg_ref[:, cols]).astype(BF16)
        m[:, :w] = (lo * lax.rsqrt(lo_ms + EPS) * lg_ref[:, cols]).astype(BF16)
        yield

    for run in project_pieces(0):
        run()
    pending = []
    for c in range(nc):
        if c + 1 < nc:
            pending += project_pieces(c + 1)
        for _ in mixer(c):
            if pending:
                pending.pop(0)()
        while pending:
            pending.pop(0)()
        pending = outproj_pieces(c)
    for run in pending:
        run()

    if not split_last:
        o_ref[...] = h_ref[...] + _rms(o_ref[...], post_g_ref[...])
    xt_ref[...] = xh_s[...]
    cvt_ref[...] = cvh_s[...]
    hl_ref[0] = hc_s[...]


def _mix(h, p, xh0, cvh0, h0, *, nb, tt):
    rows, d = h.shape
    d_lru = p["cb"].shape[1]
    nt = rows // (nb * tt)
    assert rows == nb * nt * tt
    const = lambda b, t: (0, 0)
    once = pl.Buffered(1)
    vec = pl.BlockSpec((1, d_lru), const)
    gvec = pl.BlockSpec((1, d), const)
    state8 = jax.ShapeDtypeStruct((nb * SUBLANE, d_lru), F32)
    return pl.pallas_call(
        functools.partial(_mix_kernel, d_lru=d_lru),
        grid=(nb, nt),
        in_specs=[
            pl.BlockSpec((tt, d), lambda b, t: (b * nt + t, 0)),
            gvec,
            pl.BlockSpec(p["win"].shape, const, pipeline_mode=once),
            pl.BlockSpec(p["cw"].shape, const), vec,
            pl.BlockSpec(p["wax"].shape, lambda b, t: (0, 0, 0)),
            vec, vec, vec,
            pl.BlockSpec(p["sw"].shape, const), vec, vec,
            pl.BlockSpec((2 * MXU_DIM, MXU_DIM), const),
            pl.BlockSpec(p["wout"].shape, const, pipeline_mode=once),
            gvec,
            pl.BlockSpec((SUBLANE, d_lru), const),
            pl.BlockSpec((SUBLANE, d_lru), const),
            vec,
        ],
        out_specs=[
            pl.BlockSpec((tt, d), lambda b, t: (b * nt + t, 0)),
            pl.BlockSpec((SUBLANE, d_lru), lambda b, t: (b, 0)),
            pl.BlockSpec((SUBLANE, d_lru), lambda b, t: (b, 0)),
            pl.BlockSpec((1, 1, d_lru), lambda b, t: (b, 0, 0)),
        ],
        out_shape=[
            jax.ShapeDtypeStruct((rows, d), F32),
            state8, state8,
            jax.ShapeDtypeStruct((nb, 1, d_lru), F32),
        ],
        scratch_shapes=[
            pltpu.VMEM((SUBLANE, d_lru), F32),
            pltpu.VMEM((SUBLANE, d_lru), F32),
            pltpu.VMEM((1, d_lru), F32),
            pltpu.VMEM((tt, d), BF16),
            pltpu.VMEM((tt, N_Z * MXU_DIM), F32),
            pltpu.VMEM((tt, N_Z * MXU_DIM), F32),
            pltpu.VMEM((tt, 2 * MXU_DIM), BF16),
            pltpu.VMEM((tt, 2 * MXU_DIM), BF16),
        ],
        compiler_params=pltpu.CompilerParams(
            dimension_semantics=("parallel", "arbitrary"), vmem_limit_bytes=VMEM_LIMIT),
        name="mix",
    )(h, p["pre_g"], p["win"], p["cw"], p["cb"], p["wax"], p["ba"], p["bx"], p["lam"], p["sw"], p["lg"],
      p["sg"], p["gmat"], p["wout"], p["post_g"], xh0, cvh0, h0)


def _block_diag_gates(w_a, w_x):
    nh, blk, _ = w_a.shape
    per = MXU_DIM // blk
    eye = jnp.eye(per, dtype=w_a.dtype)

    def bd(w):
        w = w.reshape(nh // per, per, blk, blk)
        return jnp.einsum("cpij,pq->cpiqj", w, eye).reshape(nh // per, MXU_DIM, MXU_DIM)

    return jnp.concatenate([bd(w_a), bd(w_x)], axis=-1).astype(BF16)


def kernel(x, meta_tokens, ffn1_pre_g, ffn1_w_gate, ffn1_w_up, ffn1_w_down, ffn1_post_g, mix_pre_g, w_in,
           lru_conv_w, lru_conv_b, lru_w_a, lru_b_a, lru_w_x, lru_b_x, lru_lambda, sconv_w, lru_out_g,
           sconv_out_g, w_out, mix_post_g, ffn2_pre_g, ffn2_w_gate, ffn2_w_up, ffn2_w_down, ffn2_post_g):
    bn, seq, d = x.shape
    n_meta = meta_tokens.shape[0]
    d_lru = lru_conv_w.shape[-1]
    d_sc = sconv_w.shape[-1]
    assert ffn1_pre_g.shape[0] == 1, "single layer"
    assert n_meta % BF16_ROWS == 0
    assert d_lru == d_sc == N_LRU_HEADS * HEAD_DIM == N_SCONV_GROUPS * HEAD_DIM
    assert lru_w_a.shape[-1] == HEAD_DIM and d_lru % MXU_DIM == 0
    assert w_in.shape[-1] == N_Z * d_lru and w_out.shape[1] == 2 * d_lru

    w1 = tuple(w[0].astype(BF16) for w in (ffn1_w_gate, ffn1_w_up, ffn1_w_down))
    w2 = tuple(w[0].astype(BF16) for w in (ffn2_w_gate, ffn2_w_up, ffn2_w_down))
    head = jnp.arange(MXU_DIM) // HEAD_DIM
    mp = {
        "pre_g": mix_pre_g,
        "win": w_in[0].astype(BF16),
        "cw": lru_conv_w[0], "cb": lru_conv_b,
        "wax": _block_diag_gates(0.5 * lru_w_a[0], 0.5 * lru_w_x[0]),
        "ba": lru_b_a, "bx": lru_b_x, "lam": lru_lambda,
        "sw": sconv_w[0], "lg": lru_out_g, "sg": sconv_out_g,
        "gmat": jnp.tile(((head[:, None] == head[None, :]).astype(F32) / HEAD_DIM).astype(BF16), (2, 1)),
        "wout": w_out[0].reshape(2, d_lru // MXU_DIM, MXU_DIM, d).transpose(1, 0, 2, 3)
                        .reshape(2 * d_lru, d).astype(BF16),
        "post_g": mix_post_g,
    }
    ffn_tiles = dict(tm=1024, tf=512, edge_chunks=(512, 512))

    h = x.reshape(bn * seq, d)
    h, hm = _ffn(h, ffn1_pre_g, *w1, ffn1_post_g, extra=meta_tokens, **ffn_tiles)
    zero8 = jnp.zeros((SUBLANE, d_lru), F32)
    _, xt, cvt, hl = _mix(hm, mp, zero8, zero8, jnp.zeros((1, d_lru), F32), nb=1, tt=n_meta)
    h, _, _, _ = _mix(h, mp, xt, cvt, hl[0], nb=bn, tt=512)
    h = _ffn(h, ffn2_pre_g, *w2, ffn2_post_g, **ffn_tiles)
    return h.reshape(bn, seq, d)
```

```python
import functools

import jax
import jax.numpy as jnp
from jax import lax
from jax.experimental import pallas as pl
from jax.experimental.pallas import tpu as pltpu

EPS = 1e-6
N_LRU_HEADS = 16
N_SCONV_GROUPS = 16
HEAD_DIM = 64
LRU_C = 8.0
SUBLANE = 8
BF16_ROWS = 2 * SUBLANE
MXU_DIM = 256
TAIL = MXU_DIM // 2
VMEM_LIMIT = 56 * 1024 * 1024
N_Z = 5

F32 = jnp.float32
BF16 = jnp.bfloat16


def _sigmoid(x):
    return 0.5 + 0.5 * jnp.tanh(0.5 * x)


def _rms(x, g):
    ms = jnp.mean(x * x, axis=-1, keepdims=True)
    return x * lax.rsqrt(ms + EPS) * g


def _silu_mul(g, u):
    return (g * _sigmoid(g) * u).astype(BF16)


def _ffn_step(h_ref, e_ref, pre_g_ref, wg_ref, wu_ref, wd_ref, wt_ref, post_g_ref, o_ref, eo_ref, xn_ref, *,
              first, last, width, chunks, extra):
    tm = h_ref.shape[0]
    r0 = 0
    for ci, rc in enumerate(chunks):
        rows = pl.ds(r0, rc)
        ride = e_ref.shape[0] if extra and ci == len(chunks) - 1 else 0
        if first:
            xn_ref[rows, :] = _rms(h_ref[rows, :], pre_g_ref[...]).astype(BF16)
            if ride:
                xn_ref[pl.ds(tm, ride), :] = _rms(e_ref[...], pre_g_ref[...]).astype(BF16)
        nr = rc + ride
        xn = xn_ref[pl.ds(r0, nr), :]
        wm = width - TAIL if last else width
        g = jnp.dot(xn, wg_ref[:, :wm], preferred_element_type=F32)
        u = jnp.dot(xn, wu_ref[:, :wm], preferred_element_type=F32)
        acc = jnp.dot(_silu_mul(g, u), wd_ref[:wm, :], preferred_element_type=F32)
        if last:
            cuts = (0, rc // 2, nr) if rc >= 2 * BF16_ROWS else (0, nr)
            gu = jnp.concatenate([jnp.dot(xn[lo:hi], wt_ref[...], preferred_element_type=F32)
                                  for lo, hi in zip(cuts[:-1], cuts[1:])], axis=0)
            acc += jnp.dot(_silu_mul(gu[:, :TAIL], gu[:, TAIL:]), wd_ref[wm:width, :],
                           preferred_element_type=F32)
        main = acc[:rc]
        if not first:
            main = o_ref[rows, :] + main
        o_ref[rows, :] = main
        if last:
            o_ref[rows, :] = h_ref[rows, :] + _rms(o_ref[rows, :], 0.5 * post_g_ref[...])
        if ride:
            ext = acc[rc:]
            if not first:
                ext = eo_ref[...] + ext
            if last:
                ext = e_ref[...] + _rms(ext, 0.5 * post_g_ref[...])
            eo_ref[...] = ext
        r0 += rc


def _ffn_kernel(*refs, rem, edge_chunks, has_extra):
    if has_extra:
        h_ref, e_ref, pre_g_ref, wg_ref, wu_ref, wd_ref, wt_ref, post_g_ref, o_ref, eo_ref, xn_ref = refs
    else:
        h_ref, pre_g_ref, wg_ref, wu_ref, wd_ref, wt_ref, post_g_ref, o_ref, xn_ref = refs
        e_ref = eo_ref = None
    i = pl.program_id(0)
    j = pl.program_id(1)
    nj = pl.num_programs(1)
    tm = h_ref.shape[0]
    tf = wg_ref.shape[1]

    def step(**kw):
        _ffn_step(h_ref, e_ref, pre_g_ref, wg_ref, wu_ref, wd_ref, wt_ref, post_g_ref, o_ref, eo_ref, xn_ref, **kw)

    def steps(tile_cond, extra):
        @pl.when(jnp.logical_and(tile_cond, j == 0))
        def _():
            step(first=True, last=False, width=tf, chunks=edge_chunks, extra=extra)

        @pl.when(jnp.logical_and(tile_cond, jnp.logical_and(j > 0, j < nj - 1)))
        def _():
            step(first=False, last=False, width=tf, chunks=(tm,), extra=extra)

        @pl.when(jnp.logical_and(tile_cond, j == nj - 1))
        def _():
            step(first=False, last=True, width=rem, chunks=edge_chunks, extra=extra)

    if has_extra:
        steps(i == 0, True)
        steps(i > 0, False)
    else:
        steps(True, False)


def _ffn(h, pre_g, wg, wu, wd, post_g, *, tm, tf, edge_chunks, extra=None):
    m, d = h.shape
    f = wg.shape[1]
    nj = pl.cdiv(f, tf)
    rem = f - (nj - 1) * tf
    assert nj >= 2 and m % tm == 0 and sum(edge_chunks) == tm
    assert rem % MXU_DIM == TAIL, "d_ff ends in half an MXU tile"
    w_tail = jnp.concatenate([wg[:, f - TAIL:], wu[:, f - TAIL:]], axis=1)
    has_extra = extra is not None
    ne = extra.shape[0] if has_extra else 0
    assert ne % BF16_ROWS == 0
    row = pl.BlockSpec((1, d), lambda i, j: (0, 0))
    e_spec = [pl.BlockSpec((ne, d), lambda i, j: (0, 0))] if has_extra else []
    out = pl.pallas_call(
        functools.partial(_ffn_kernel, rem=rem, edge_chunks=edge_chunks, has_extra=has_extra),
        grid=(m // tm, nj),
        in_specs=[pl.BlockSpec((tm, d), lambda i, j: (i, 0))] + e_spec + [
            row,
            pl.BlockSpec((d, tf), lambda i, j: (0, j)),
            pl.BlockSpec((d, tf), lambda i, j: (0, j)),
            pl.BlockSpec((tf, d), lambda i, j: (j, 0)),
            pl.BlockSpec((d, 2 * TAIL), lambda i, j: (0, 0), pipeline_mode=pl.Buffered(1)),
            row,
        ],
        out_specs=[pl.BlockSpec((tm, d), lambda i, j: (i, 0))] + e_spec,
        out_shape=[jax.ShapeDtypeStruct((m, d), F32)] + ([jax.ShapeDtypeStruct((ne, d), F32)] if has_extra else []),
        scratch_shapes=[pltpu.VMEM((tm + ne, d), BF16)],
        compiler_params=pltpu.CompilerParams(
            dimension_semantics=("arbitrary" if has_extra else "parallel", "arbitrary"),
            vmem_limit_bytes=VMEM_LIMIT),
        name="ffn",
    )(h, *([extra] if has_extra else []), pre_g, wg, wu, wd, w_tail, post_g)
    return out if has_extra else out[0]


def _shift_rows(x, k, head):
    r = pltpu.roll(x, k, 0)
    hr = pltpu.roll(head, k, 0)
    row = lax.broadcasted_iota(jnp.int32, head.shape, 0)
    first = jnp.where(row < k, hr, r[:SUBLANE])
    if x.shape[0] == SUBLANE:
        return first
    return jnp.concatenate([first, r[SUBLANE:]], axis=0)


def _shift_fill(x, s, fill):
    tt = x.shape[0]
    r = pltpu.roll(x, s, 0)
    row = lax.broadcasted_iota(jnp.int32, (SUBLANE,) + x.shape[1:], 0)
    first = jnp.where(row < s, fill, r[:SUBLANE])
    if tt == SUBLANE:
        return first
    return jnp.concatenate([first, r[SUBLANE:]], axis=0)


def _window_steps(a, u, steps):
    for s in steps:
        u = u + a * _shift_fill(u, s, 0.0)
        a = a * _shift_fill(a, s, 1.0)
    return a, u


def _scan_groups(a, u, h_in):
    hs = []
    h = h_in
    for g in range(a.shape[0] // SUBLANE):
        rows = slice(g * SUBLANE, (g + 1) * SUBLANE)
        h = a[rows] * h + u[rows]
        hs.append(h)
    return jnp.concatenate(hs, axis=0)


def _group_mean_sq(y, gmat):
    y2 = y * y
    hi = y2.astype(BF16)
    lo = (y2 - hi.astype(F32)).astype(BF16)
    return jnp.dot(jnp.concatenate([hi, lo], axis=1), gmat, preferred_element_type=F32)


def _gelu_tanh(x):
    c = 0.7978845608028654
    hx = 0.5 * x
    return hx + hx * jnp.tanh(x * (c + (c * 0.044715) * (x * x)))


def _mix_kernel(h_ref, pre_g_ref, win_ref, cw_ref, cb_ref, wax_ref, ba_ref, bx_ref, lam_ref, sw_ref, lg_ref,
                sg_ref, gmat_ref, wout_ref, post_g_ref, xh0_ref, cvh0_ref, h0_ref,
                o_ref, xt_ref, cvt_ref, hl_ref,
                xh_s, cvh_s, hc_s, un_s, z0_s, z1_s, m0_s, m1_s, *, d_lru, states_only):
    @pl.when(pl.program_id(1) == 0)
    def _():
        xh_s[...] = xh0_ref[...]
        cvh_s[...] = cvh0_ref[...]
        hc_s[...] = h0_ref[...]

    tt, d = h_ref.shape
    w = MXU_DIM
    nc = d_lru // w
    z_slots = (z0_s, z1_s)
    m_slots = (m0_s, m1_s)
    split_last = tt >= 2 * BF16_ROWS
    gmat = gmat_ref[...]
    un_s[...] = _rms(h_ref[...], pre_g_ref[...]).astype(BF16)

    def project_pieces(c):
        def piece(k):
            def run():
                col = k * d_lru + c * w
                z_slots[c % 2][:, k * w:(k + 1) * w] = jnp.dot(
                    un_s[...], win_ref[:, col:col + w], preferred_element_type=F32)
            return run
        return [piece(k) for k in range(N_Z)]

    def outproj_pieces(c):
        half = d // 2
        wrows = slice(c * 2 * w, (c + 1) * 2 * w)

        def piece(n):
            def run():
                ncols = slice(n * half, (n + 1) * half)
                part = jnp.dot(m_slots[c % 2][...], wout_ref[wrows, ncols], preferred_element_type=F32)
                if c == 0:
                    o_ref[:, ncols] = part
                else:
                    o_ref[:, ncols] += part
            return run

        def last_piece(r):
            def run():
                rows = pl.ds(r * (tt // 2), tt // 2)
                acc = jnp.dot(m_slots[c % 2][rows, :], wout_ref[wrows, :], preferred_element_type=F32)
                if c > 0:
                    acc = o_ref[rows, :] + acc
                o_ref[rows, :] = h_ref[rows, :] + _rms(acc, post_g_ref[...])
            return run

        if c == nc - 1 and split_last:
            return [last_piece(0), last_piece(1)]
        return [piece(0), piece(1)]

    def mixer(c):
        z = z_slots[c % 2]
        m = m_slots[c % 2]
        cols = slice(c * w, (c + 1) * w)
        x = z[:, w:2 * w]
        head = xh_s[:, cols]
        xc = (cb_ref[:, cols] + cw_ref[3:4, cols] * x
              + cw_ref[2:3, cols] * _shift_rows(x, 1, head)
              + cw_ref[1:2, cols] * _shift_rows(x, 2, head)
              + cw_ref[0:1, cols] * _shift_rows(x, 3, head))
        xh_s[:, cols] = x[tt - SUBLANE:]
        pre = jnp.dot(xc.astype(BF16), wax_ref[c], preferred_element_type=F32)
        yield
        cv = z[:, 3 * w:4 * w] * z[:, 4 * w:5 * w]
        chead = cvh_s[:, cols]
        sc = z[:, 2 * w:3 * w] * (sw_ref[2:3, cols] * cv
                                  + sw_ref[1:2, cols] * _shift_rows(cv, 1, chead)
                                  + sw_ref[0:1, cols] * _shift_rows(cv, 2, chead))
        cvh_s[:, cols] = cv[tt - SUBLANE:]
        sc_ms = _group_mean_sq(sc, gmat)
        yield
        ta = jnp.tanh(pre[:, :w] + 0.5 * ba_ref[:, cols])
        gx = 0.5 + 0.5 * jnp.tanh(pre[:, w:] + 0.5 * bx_ref[:, cols])
        nl = -lam_ref[:, cols]
        sp = jnp.maximum(nl, 0.0) + jnp.log1p(jnp.exp(-jnp.abs(nl)))
        hc = (-0.5 * LRU_C) * sp
        log_a = hc + hc * ta
        a = jnp.exp(log_a)
        m2 = -jnp.tanh(log_a) * (a * a + 1.0)
        mult = jnp.where(m2 > 0.0, m2 * lax.rsqrt(m2), 0.0)
        u = mult * (gx * xc)
        yield
        a, u = _window_steps(a, u, (1, 2))
        yield
        a, u = _window_steps(a, u, (4,))
        h = _scan_groups(a, u, hc_s[:, cols])
        hc_s[:, cols] = h[tt - 1:]
        yield
        lo = h * _gelu_tanh(z[:, 0:w])
        lo_ms = _group_mean_sq(lo, gmat)
        yield
        m[:, w:] = (sc * lax.rsqrt(sc_ms + EPS) * sg_ref[:, cols]).astype(BF16)
        m[:, :w] = (lo * lax.rsqrt(lo_ms + EPS) * lg_ref[:, cols]).astype(BF16)
        yield

    for run in project_pieces(0):
        run()
    pending = []
    for c in range(nc):
        if c + 1 < nc:
            pending += project_pieces(c + 1)
        for _ in mixer(c):
            if pending:
                pending.pop(0)()
        while pending:
            pending.pop(0)()
        pending = [] if states_only else outproj_pieces(c)
    for run in pending:
        run()

    if states_only:
        o_ref[...] = h_ref[...]
    elif not split_last:
        o_ref[...] = h_ref[...] + _rms(o_ref[...], post_g_ref[...])
    xt_ref[...] = xh_s[...]
    cvt_ref[...] = cvh_s[...]
    hl_ref[0] = hc_s[...]


def _mix(h, p, xh0, cvh0, h0, *, nb, tt, states_only=False):
    rows, d = h.shape
    d_lru = p["cb"].shape[1]
    nt = rows // (nb * tt)
    assert rows == nb * nt * tt
    wout = p["wout"][:2 * MXU_DIM] if states_only else p["wout"]
    const = lambda b, t: (0, 0)
    once = pl.Buffered(1)
    vec = pl.BlockSpec((1, d_lru), const)
    gvec = pl.BlockSpec((1, d), const)
    state8 = jax.ShapeDtypeStruct((nb * SUBLANE, d_lru), F32)
    return pl.pallas_call(
        functools.partial(_mix_kernel, d_lru=d_lru, states_only=states_only),
        grid=(nb, nt),
        in_specs=[
            pl.BlockSpec((tt, d), lambda b, t: (b * nt + t, 0)),
            gvec,
            pl.BlockSpec(p["win"].shape, const, pipeline_mode=once),
            pl.BlockSpec(p["cw"].shape, const), vec,
            pl.BlockSpec(p["wax"].shape, lambda b, t: (0, 0, 0)),
            vec, vec, vec,
            pl.BlockSpec(p["sw"].shape, const), vec, vec,
            pl.BlockSpec((2 * MXU_DIM, MXU_DIM), const),
            pl.BlockSpec(wout.shape, const, pipeline_mode=once),
            gvec,
            pl.BlockSpec((SUBLANE, d_lru), const),
            pl.BlockSpec((SUBLANE, d_lru), const),
            vec,
        ],
        out_specs=[
            pl.BlockSpec((tt, d), lambda b, t: (b * nt + t, 0)),
            pl.BlockSpec((SUBLANE, d_lru), lambda b, t: (b, 0)),
            pl.BlockSpec((SUBLANE, d_lru), lambda b, t: (b, 0)),
            pl.BlockSpec((1, 1, d_lru), lambda b, t: (b, 0, 0)),
        ],
        out_shape=[
            jax.ShapeDtypeStruct((rows, d), F32),
            state8, state8,
            jax.ShapeDtypeStruct((nb, 1, d_lru), F32),
        ],
        scratch_shapes=[
            pltpu.VMEM((SUBLANE, d_lru), F32),
            pltpu.VMEM((SUBLANE, d_lru), F32),
            pltpu.VMEM((1, d_lru), F32),
            pltpu.VMEM((tt, d), BF16),
            pltpu.VMEM((tt, N_Z * MXU_DIM), F32),
            pltpu.VMEM((tt, N_Z * MXU_DIM), F32),
            pltpu.VMEM((tt, 2 * MXU_DIM), BF16),
            pltpu.VMEM((tt, 2 * MXU_DIM), BF16),
        ],
        compiler_params=pltpu.CompilerParams(
            dimension_semantics=("parallel", "arbitrary"), vmem_limit_bytes=VMEM_LIMIT),
        name="mix",
    )(h, p["pre_g"], p["win"], p["cw"], p["cb"], p["wax"], p["ba"], p["bx"], p["lam"], p["sw"], p["lg"],
      p["sg"], p["gmat"], wout, p["post_g"], xh0, cvh0, h0)


def _block_diag_gates(w_a, w_x):
    nh, blk, _ = w_a.shape
    per = MXU_DIM // blk
    eye = jnp.eye(per, dtype=w_a.dtype)

    def bd(w):
        w = w.reshape(nh // per, per, blk, blk)
        return jnp.einsum("cpij,pq->cpiqj", w, eye).reshape(nh // per, MXU_DIM, MXU_DIM)

    return jnp.concatenate([bd(w_a), bd(w_x)], axis=-1).astype(BF16)


def kernel(x, meta_tokens, ffn1_pre_g, ffn1_w_gate, ffn1_w_up, ffn1_w_down, ffn1_post_g, mix_pre_g, w_in,
           lru_conv_w, lru_conv_b, lru_w_a, lru_b_a, lru_w_x, lru_b_x, lru_lambda, sconv_w, lru_out_g,
           sconv_out_g, w_out, mix_post_g, ffn2_pre_g, ffn2_w_gate, ffn2_w_up, ffn2_w_down, ffn2_post_g):
    bn, seq, d = x.shape
    n_meta = meta_tokens.shape[0]
    d_lru = lru_conv_w.shape[-1]
    d_sc = sconv_w.shape[-1]
    assert ffn1_pre_g.shape[0] == 1, "single layer"
    assert n_meta % BF16_ROWS == 0
    assert d_lru == d_sc == N_LRU_HEADS * HEAD_DIM == N_SCONV_GROUPS * HEAD_DIM
    assert lru_w_a.shape[-1] == HEAD_DIM and d_lru % MXU_DIM == 0
    assert w_in.shape[-1] == N_Z * d_lru and w_out.shape[1] == 2 * d_lru

    w1 = tuple(w[0].astype(BF16) for w in (ffn1_w_gate, ffn1_w_up, ffn1_w_down))
    w2 = tuple(w[0].astype(BF16) for w in (ffn2_w_gate, ffn2_w_up, ffn2_w_down))
    head = jnp.arange(MXU_DIM) // HEAD_DIM
    mp = {
        "pre_g": mix_pre_g,
        "win": w_in[0].astype(BF16),
        "cw": lru_conv_w[0], "cb": lru_conv_b,
        "wax": _block_diag_gates(0.5 * lru_w_a[0], 0.5 * lru_w_x[0]),
        "ba": lru_b_a, "bx": lru_b_x, "lam": lru_lambda,
        "sw": sconv_w[0], "lg": lru_out_g, "sg": sconv_out_g,
        "gmat": jnp.tile(((head[:, None] == head[None, :]).astype(F32) / HEAD_DIM).astype(BF16), (2, 1)),
        "wout": w_out[0].reshape(2, d_lru // MXU_DIM, MXU_DIM, d).transpose(1, 0, 2, 3)
                        .reshape(2 * d_lru, d).astype(BF16),
        "post_g": mix_post_g,
    }
    ffn_tiles = dict(tm=1024, tf=512, edge_chunks=(512, 512))

    h = x.reshape(bn * seq, d)
    h, hm = _ffn(h, ffn1_pre_g, *w1, ffn1_post_g, extra=meta_tokens, **ffn_tiles)
    zero8 = jnp.zeros((SUBLANE, d_lru), F32)
    _, xt, cvt, hl = _mix(hm, mp, zero8, zero8, jnp.zeros((1, d_lru), F32), nb=1, tt=n_meta,
                          states_only=True)
    h, _, _, _ = _mix(h, mp, xt, cvt, hl[0], nb=bn, tt=512)
    h = _ffn(h, ffn2_pre_g, *w2, ffn2_post_g, **ffn_tiles)
    return h.reshape(bn, seq, d)
```

```python
import functools

import jax
import jax.numpy as jnp
from jax import lax
from jax.experimental import pallas as pl
from jax.experimental.pallas import tpu as pltpu

EPS = 1e-6
N_LRU_HEADS = 16
N_SCONV_GROUPS = 16
HEAD_DIM = 64
LRU_C = 8.0
SUBLANE = 8
BF16_ROWS = 2 * SUBLANE
MXU_DIM = 256
TAIL = MXU_DIM // 2
VMEM_LIMIT = 56 * 1024 * 1024
N_Z = 5

F32 = jnp.float32
BF16 = jnp.bfloat16


def _sigmoid(x):
    return 0.5 + 0.5 * jnp.tanh(0.5 * x)


def _rms(x, g):
    ms = jnp.mean(x * x, axis=-1, keepdims=True)
    return x * lax.rsqrt(ms + EPS) * g


def _silu_mul(g, u):
    return (g * _sigmoid(g) * u).astype(BF16)


def _ffn_step(h_ref, e_ref, pre_g_ref, wg_ref, wu_ref, wd_ref, wt_ref, post_g_ref, o_ref, eo_ref, xn_ref, *,
              first, last, width, chunks, extra):
    tm = h_ref.shape[0]
    r0 = 0
    for ci, rc in enumerate(chunks):
        rows = pl.ds(r0, rc)
        ride = e_ref.shape[0] if extra and ci == len(chunks) - 1 else 0
        if first:
            xn_ref[rows, :] = _rms(h_ref[rows, :], pre_g_ref[...]).astype(BF16)
            if ride:
                xn_ref[pl.ds(tm, ride), :] = _rms(e_ref[...], pre_g_ref[...]).astype(BF16)
        nr = rc + ride
        xn = xn_ref[pl.ds(r0, nr), :]
        wm = width - TAIL if last else width
        g = jnp.dot(xn, wg_ref[:, :wm], preferred_element_type=F32)
        u = jnp.dot(xn, wu_ref[:, :wm], preferred_element_type=F32)
        acc = jnp.dot(_silu_mul(g, u), wd_ref[:wm, :], preferred_element_type=F32)
        if last:
            cuts = (0, rc // 2, nr) if rc >= 2 * BF16_ROWS else (0, nr)
            gu = jnp.concatenate([jnp.dot(xn[lo:hi], wt_ref[...], preferred_element_type=F32)
                                  for lo, hi in zip(cuts[:-1], cuts[1:])], axis=0)
            acc += jnp.dot(_silu_mul(gu[:, :TAIL], gu[:, TAIL:]), wd_ref[wm:width, :],
                           preferred_element_type=F32)
        main = acc[:rc]
        if not first:
            main = o_ref[rows, :] + main
        o_ref[rows, :] = main
        if last:
            o_ref[rows, :] = h_ref[rows, :] + _rms(o_ref[rows, :], 0.5 * post_g_ref[...])
        if ride:
            ext = acc[rc:]
            if not first:
                ext = eo_ref[...] + ext
            if last:
                ext = e_ref[...] + _rms(ext, 0.5 * post_g_ref[...])
            eo_ref[...] = ext
        r0 += rc


def _ffn_kernel(*refs, rem, edge_chunks, has_extra):
    if has_extra:
        h_ref, e_ref, pre_g_ref, wg_ref, wu_ref, wd_ref, wt_ref, post_g_ref, o_ref, eo_ref, xn_ref = refs
    else:
        h_ref, pre_g_ref, wg_ref, wu_ref, wd_ref, wt_ref, post_g_ref, o_ref, xn_ref = refs
        e_ref = eo_ref = None
    i = pl.program_id(0)
    j = pl.program_id(1)
    nj = pl.num_programs(1)
    tm = h_ref.shape[0]
    tf = wg_ref.shape[1]

    def step(**kw):
        _ffn_step(h_ref, e_ref, pre_g_ref, wg_ref, wu_ref, wd_ref, wt_ref, post_g_ref, o_ref, eo_ref, xn_ref, **kw)

    def steps(tile_cond, extra):
        @pl.when(jnp.logical_and(tile_cond, j == 0))
        def _():
            step(first=True, last=False, width=tf, chunks=edge_chunks, extra=extra)

        @pl.when(jnp.logical_and(tile_cond, jnp.logical_and(j > 0, j < nj - 1)))
        def _():
            step(first=False, last=False, width=tf, chunks=(tm,), extra=extra)

        @pl.when(jnp.logical_and(tile_cond, j == nj - 1))
        def _():
            step(first=False, last=True, width=rem, chunks=edge_chunks, extra=extra)

    if has_extra:
        steps(i == 0, True)
        steps(i > 0, False)
    else:
        steps(True, False)


def _ffn(h, pre_g, wg, wu, wd, post_g, *, tm, tf, edge_chunks, extra=None):
    m, d = h.shape
    f = wg.shape[1]
    nj = pl.cdiv(f, tf)
    rem = f - (nj - 1) * tf
    assert nj >= 2 and m % tm == 0 and sum(edge_chunks) == tm
    assert rem % MXU_DIM == TAIL, "d_ff ends in half an MXU tile"
    w_tail = jnp.concatenate([wg[:, f - TAIL:], wu[:, f - TAIL:]], axis=1)
    has_extra = extra is not None
    ne = extra.shape[0] if has_extra else 0
    assert ne % BF16_ROWS == 0
    row = pl.BlockSpec((1, d), lambda i, j: (0, 0))
    e_spec = [pl.BlockSpec((ne, d), lambda i, j: (0, 0))] if has_extra else []
    out = pl.pallas_call(
        functools.partial(_ffn_kernel, rem=rem, edge_chunks=edge_chunks, has_extra=has_extra),
        grid=(m // tm, nj),
        in_specs=[pl.BlockSpec((tm, d), lambda i, j: (i, 0))] + e_spec + [
            row,
            pl.BlockSpec((d, tf), lambda i, j: (0, j)),
            pl.BlockSpec((d, tf), lambda i, j: (0, j)),
            pl.BlockSpec((tf, d), lambda i, j: (j, 0)),
            pl.BlockSpec((d, 2 * TAIL), lambda i, j: (0, 0), pipeline_mode=pl.Buffered(1)),
            row,
        ],
        out_specs=[pl.BlockSpec((tm, d), lambda i, j: (i, 0))] + e_spec,
        out_shape=[jax.ShapeDtypeStruct((m, d), F32)] + ([jax.ShapeDtypeStruct((ne, d), F32)] if has_extra else []),
        scratch_shapes=[pltpu.VMEM((tm + ne, d), BF16)],
        compiler_params=pltpu.CompilerParams(
            dimension_semantics=("arbitrary" if has_extra else "parallel", "arbitrary"),
            vmem_limit_bytes=VMEM_LIMIT),
        name="ffn",
    )(h, *([extra] if has_extra else []), pre_g, wg, wu, wd, w_tail, post_g)
    return out if has_extra else out[0]


def _shift_rows(x, k, head):
    r = pltpu.roll(x, k, 0)
    hr = pltpu.roll(head, k, 0)
    row = lax.broadcasted_iota(jnp.int32, head.shape, 0)
    first = jnp.where(row < k, hr, r[:SUBLANE])
    if x.shape[0] == SUBLANE:
        return first
    return jnp.concatenate([first, r[SUBLANE:]], axis=0)


def _shift_fill(x, s, fill):
    tt = x.shape[0]
    r = pltpu.roll(x, s, 0)
    row = lax.broadcasted_iota(jnp.int32, (SUBLANE,) + x.shape[1:], 0)
    first = jnp.where(row < s, fill, r[:SUBLANE])
    if tt == SUBLANE:
        return first
    return jnp.concatenate([first, r[SUBLANE:]], axis=0)


def _window_steps(a, u, steps):
    for s in steps:
        u = u + a * _shift_fill(u, s, 0.0)
        a = a * _shift_fill(a, s, 1.0)
    return a, u


def _scan_groups(a, u, h_in):
    hs = []
    h = h_in
    for g in range(a.shape[0] // SUBLANE):
        rows = slice(g * SUBLANE, (g + 1) * SUBLANE)
        h = a[rows] * h + u[rows]
        hs.append(h)
    return jnp.concatenate(hs, axis=0)


def _group_mean_sq(y, gmat):
    y2 = y * y
    hi = y2.astype(BF16)
    lo = (y2 - hi.astype(F32)).astype(BF16)
    return jnp.dot(jnp.concatenate([hi, lo], axis=1), gmat, preferred_element_type=F32)


def _gelu_tanh(x):
    return 0.5 * x * (1.0 + jnp.tanh(0.7978845608028654 * (x + 0.044715 * (x * x * x))))


def _mix_kernel(h_ref, pre_g_ref, win_ref, cw_ref, cb_ref, wax_ref, ba_ref, bx_ref, lam_ref, sw_ref, lg_ref,
                sg_ref, gmat_ref, wout_ref, post_g_ref, xh0_ref, cvh0_ref, h0_ref,
                o_ref, xt_ref, cvt_ref, hl_ref,
                xh_s, cvh_s, hc_s, un_s, z0_s, z1_s, m0_s, m1_s, *, d_lru, states_only):
    @pl.when(pl.program_id(1) == 0)
    def _():
        xh_s[...] = xh0_ref[...]
        cvh_s[...] = cvh0_ref[...]
        hc_s[...] = h0_ref[...]

    tt, d = h_ref.shape
    w = MXU_DIM
    nc = d_lru // w
    z_slots = (z0_s, z1_s)
    m_slots = (m0_s, m1_s)
    split_last = tt >= 2 * BF16_ROWS
    gmat = gmat_ref[...]
    un_s[...] = _rms(h_ref[...], pre_g_ref[...]).astype(BF16)

    def project_pieces(c):
        def piece(k):
            def run():
                col = k * d_lru + c * w
                z_slots[c % 2][:, k * w:(k + 1) * w] = jnp.dot(
                    un_s[...], win_ref[:, col:col + w], preferred_element_type=F32)
            return run
        return [piece(k) for k in range(N_Z)]

    def outproj_pieces(c):
        half = d // 2
        wrows = slice(c * 2 * w, (c + 1) * 2 * w)

        def piece(n):
            def run():
                ncols = slice(n * half, (n + 1) * half)
                part = jnp.dot(m_slots[c % 2][...], wout_ref[wrows, ncols], preferred_element_type=F32)
                if c == 0:
                    o_ref[:, ncols] = part
                else:
                    o_ref[:, ncols] += part
            return run

        def last_piece(r):
            def run():
                rows = pl.ds(r * (tt // 2), tt // 2)
                acc = jnp.dot(m_slots[c % 2][rows, :], wout_ref[wrows, :], preferred_element_type=F32)
                if c > 0:
                    acc = o_ref[rows, :] + acc
                o_ref[rows, :] = h_ref[rows, :] + _rms(acc, post_g_ref[...])
            return run

        if c == nc - 1 and split_last:
            return [last_piece(0), last_piece(1)]
        return [piece(0), piece(1)]

    def mixer(c):
        z = z_slots[c % 2]
        m = m_slots[c % 2]
        cols = slice(c * w, (c + 1) * w)
        x = z[:, w:2 * w]
        head = xh_s[:, cols]
        xc = (cb_ref[:, cols] + cw_ref[3:4, cols] * x
              + cw_ref[2:3, cols] * _shift_rows(x, 1, head)
              + cw_ref[1:2, cols] * _shift_rows(x, 2, head)
              + cw_ref[0:1, cols] * _shift_rows(x, 3, head))
        xh_s[:, cols] = x[tt - SUBLANE:]
        pre = jnp.dot(xc.astype(BF16), wax_ref[c], preferred_element_type=F32)
        yield
        cv = z[:, 3 * w:4 * w] * z[:, 4 * w:5 * w]
        chead = cvh_s[:, cols]
        sc = z[:, 2 * w:3 * w] * (sw_ref[2:3, cols] * cv
                                  + sw_ref[1:2, cols] * _shift_rows(cv, 1, chead)
                                  + sw_ref[0:1, cols] * _shift_rows(cv, 2, chead))
        cvh_s[:, cols] = cv[tt - SUBLANE:]
        sc_ms = _group_mean_sq(sc, gmat)
        yield
        ga = 0.5 + 0.5 * jnp.tanh(pre[:, :w] + 0.5 * ba_ref[:, cols])
        gx = 0.5 + 0.5 * jnp.tanh(pre[:, w:] + 0.5 * bx_ref[:, cols])
        nl = -lam_ref[:, cols]
        sp = jnp.maximum(nl, 0.0) + jnp.log1p(jnp.exp(-jnp.abs(nl)))
        log_a = ga * (-LRU_C * sp)
        a = jnp.exp(log_a)
        m2 = -jnp.tanh(log_a) * (a * a + 1.0)
        mult = jnp.where(m2 > 0.0, m2 * lax.rsqrt(m2), 0.0)
        u = mult * (gx * xc)
        yield
        a, u = _window_steps(a, u, (1, 2))
        yield
        a, u = _window_steps(a, u, (4,))
        h = _scan_groups(a, u, hc_s[:, cols])
        hc_s[:, cols] = h[tt - 1:]
        yield
        lo = h * _gelu_tanh(z[:, 0:w])
        lo_ms = _group_mean_sq(lo, gmat)
        yield
        m[:, w:] = (sc * lax.rsqrt(sc_ms + EPS) * sg_ref[:, cols]).astype(BF16)
        m[:, :w] = (lo * lax.rsqrt(lo_ms + EPS) * lg_ref[:, cols]).astype(BF16)
        yield

    for run in project_pieces(0):
        run()
    pending = []
    for c in range(nc):
        if c + 1 < nc:
            pending += project_pieces(c + 1)
        for _ in mixer(c):
            if pending:
                pending.pop(0)()
        while pending:
            pending.pop(0)()
        pending = [] if states_only else outproj_pieces(c)
    for run in pending:
        run()

    if states_only:
        o_ref[...] = h_ref[...]
    elif not split_last:
        o_ref[...] = h_ref[...] + _rms(o_ref[...], post_g_ref[...])
    xt_ref[...] = xh_s[...]
    cvt_ref[...] = cvh_s[...]
    hl_ref[0] = hc_s[...]


def _mix(h, p, xh0, cvh0, h0, *, nb, tt, states_only=False):
    rows, d = h.shape
    d_lru = p["cb"].shape[1]
    nt = rows // (nb * tt)
    assert rows == nb * nt * tt
    wout = p["wout"][:2 * MXU_DIM] if states_only else p["wout"]
    const = lambda b, t: (0, 0)
    once = pl.Buffered(1)
    vec = pl.BlockSpec((1, d_lru), const)
    gvec = pl.BlockSpec((1, d), const)
    state8 = jax.ShapeDtypeStruct((nb * SUBLANE, d_lru), F32)
    return pl.pallas_call(
        functools.partial(_mix_kernel, d_lru=d_lru, states_only=states_only),
        grid=(nb, nt),
        in_specs=[
            pl.BlockSpec((tt, d), lambda b, t: (b * nt + t, 0)),
            gvec,
            pl.BlockSpec(p["win"].shape, const, pipeline_mode=once),
            pl.BlockSpec(p["cw"].shape, const), vec,
            pl.BlockSpec(p["wax"].shape, lambda b, t: (0, 0, 0)),
            vec, vec, vec,
            pl.BlockSpec(p["sw"].shape, const), vec, vec,
            pl.BlockSpec((2 * MXU_DIM, MXU_DIM), const),
            pl.BlockSpec(wout.shape, const, pipeline_mode=once),
            gvec,
            pl.BlockSpec((SUBLANE, d_lru), const),
            pl.BlockSpec((SUBLANE, d_lru), const),
            vec,
        ],
        out_specs=[
            pl.BlockSpec((tt, d), lambda b, t: (b * nt + t, 0)),
            pl.BlockSpec((SUBLANE, d_lru), lambda b, t: (b, 0)),
            pl.BlockSpec((SUBLANE, d_lru), lambda b, t: (b, 0)),
            pl.BlockSpec((1, 1, d_lru), lambda b, t: (b, 0, 0)),
        ],
        out_shape=[
            jax.ShapeDtypeStruct((rows, d), F32),
            state8, state8,
            jax.ShapeDtypeStruct((nb, 1, d_lru), F32),
        ],
        scratch_shapes=[
            pltpu.VMEM((SUBLANE, d_lru), F32),
            pltpu.VMEM((SUBLANE, d_lru), F32),
            pltpu.VMEM((1, d_lru), F32),
            pltpu.VMEM((tt, d), BF16),
            pltpu.VMEM((tt, N_Z * MXU_DIM), F32),
            pltpu.VMEM((tt, N_Z * MXU_DIM), F32),
            pltpu.VMEM((tt, 2 * MXU_DIM), BF16),
            pltpu.VMEM((tt, 2 * MXU_DIM), BF16),
        ],
        compiler_params=pltpu.CompilerParams(
            dimension_semantics=("parallel", "arbitrary"), vmem_limit_bytes=VMEM_LIMIT),
        name="mix",
    )(h, p["pre_g"], p["win"], p["cw"], p["cb"], p["wax"], p["ba"], p["bx"], p["lam"], p["sw"], p["lg"],
      p["sg"], p["gmat"], wout, p["post_g"], xh0, cvh0, h0)


def _block_diag_gates(w_a, w_x):
    nh, blk, _ = w_a.shape
    per = MXU_DIM // blk
    eye = jnp.eye(per, dtype=w_a.dtype)

    def bd(w):
        w = w.reshape(nh // per, per, blk, blk)
        return jnp.einsum("cpij,pq->cpiqj", w, eye).reshape(nh // per, MXU_DIM, MXU_DIM)

    return jnp.concatenate([bd(w_a), bd(w_x)], axis=-1).astype(BF16)


def kernel(x, meta_tokens, ffn1_pre_g, ffn1_w_gate, ffn1_w_up, ffn1_w_down, ffn1_post_g, mix_pre_g, w_in,
           lru_conv_w, lru_conv_b, lru_w_a, lru_b_a, lru_w_x, lru_b_x, lru_lambda, sconv_w, lru_out_g,
           sconv_out_g, w_out, mix_post_g, ffn2_pre_g, ffn2_w_gate, ffn2_w_up, ffn2_w_down, ffn2_post_g):
    bn, seq, d = x.shape
    n_meta = meta_tokens.shape[0]
    d_lru = lru_conv_w.shape[-1]
    d_sc = sconv_w.shape[-1]
    assert ffn1_pre_g.shape[0] == 1, "single layer"
    assert n_meta % BF16_ROWS == 0
    assert d_lru == d_sc == N_LRU_HEADS * HEAD_DIM == N_SCONV_GROUPS * HEAD_DIM
    assert lru_w_a.shape[-1] == HEAD_DIM and d_lru % MXU_DIM == 0
    assert w_in.shape[-1] == N_Z * d_lru and w_out.shape[1] == 2 * d_lru

    w1 = tuple(w[0].astype(BF16) for w in (ffn1_w_gate, ffn1_w_up, ffn1_w_down))
    w2 = tuple(w[0].astype(BF16) for w in (ffn2_w_gate, ffn2_w_up, ffn2_w_down))
    head = jnp.arange(MXU_DIM) // HEAD_DIM
    mp = {
        "pre_g": mix_pre_g,
        "win": w_in[0].astype(BF16),
        "cw": lru_conv_w[0], "cb": lru_conv_b,
        "wax": _block_diag_gates(0.5 * lru_w_a[0], 0.5 * lru_w_x[0]),
        "ba": lru_b_a, "bx": lru_b_x, "lam": lru_lambda,
        "sw": sconv_w[0], "lg": lru_out_g, "sg": sconv_out_g,
        "gmat": jnp.tile(((head[:, None] == head[None, :]).astype(F32) / HEAD_DIM).astype(BF16), (2, 1)),
        "wout": w_out[0].reshape(2, d_lru // MXU_DIM, MXU_DIM, d).transpose(1, 0, 2, 3)
                        .reshape(2 * d_lru, d).astype(BF16),
        "post_g": mix_post_g,
    }
    ffn_tiles = dict(tm=1024, tf=512, edge_chunks=(512, 512))

    h = x.reshape(bn * seq, d)
    h, hm = _ffn(h, ffn1_pre_g, *w1, ffn1_post_g, extra=meta_tokens, **ffn_tiles)
    zero8 = jnp.zeros((SUBLANE, d_lru), F32)
    _, xt, cvt, hl = _mix(hm, mp, zero8, zero8, jnp.zeros((1, d_lru), F32), nb=1, tt=n_meta,
                          states_only=True)
    h, _, _, _ = _mix(h, mp, xt, cvt, hl[0], nb=bn, tt=512)
    h = _ffn(h, ffn2_pre_g, *w2, ffn2_post_g, **ffn_tiles)
    return h.reshape(bn, seq, d)
```
